```python
import jax, jax.numpy as jnp
from jax import lax
import numpy as np

D_MODEL = 2048
BATCH = 1
SEQ = 8192
DEPTH = 2

GRID_W = 64
CTX_LEN = 256
N_HEADS_NA = 8
HEAD_DIM = 128
D_NA = N_HEADS_NA * HEAD_DIM
WIN_R = 8
WIN_C = 16
N_FOURIER_GROUPS = 4
FOURIER_GROUP = 128
D_FOURIER = N_FOURIER_GROUPS * FOURIER_GROUP
D_CONV = 512
CONV_WIDTH = 31
D_FF = 5632
D_IN = 3 * D_NA + D_FOURIER + 2 * D_CONV
N_BRANCH = 3
N_MOD = 9
ALPHA = (2 * DEPTH) ** 0.25
BETA = (8 * DEPTH) ** -0.25
LN_EPS = 1e-5
NEG_INF = -1e30

kernel_name = "hybrid_na_fnet_conformer_diffusion_block"


def layer_norm(x, g, b):
    xf = x.astype(jnp.float32)
    mu = xf.mean(-1, keepdims=True)
    var = jnp.square(xf - mu).mean(-1, keepdims=True)
    y = (xf - mu) * lax.rsqrt(var + LN_EPS)
    return (y * g.astype(jnp.float32) + b.astype(jnp.float32)).astype(x.dtype)


def modulate(x, shift, scale):
    return x * (1 + scale) + shift


def swiglu(h, w1, w3, w2):
    return (jax.nn.silu(h @ w1) * (h @ w3)) @ w2


def ffn_sublayer(z, m, j, w1, w3, w2, g, b):
    h = modulate(z, m[:, :, 3 * j], m[:, :, 3 * j + 1])
    y = swiglu(h, w1, w3, w2)
    return layer_norm(ALPHA * z + 0.5 * m[:, :, 3 * j + 2] * y, g, b)


def split_in(z):
    b_, n, _ = z.shape
    heads = lambda t: t.reshape(b_, n, N_HEADS_NA, HEAD_DIM)
    q = heads(z[..., 0:D_NA])
    k = heads(z[..., D_NA:2 * D_NA])
    v = heads(z[..., 2 * D_NA:3 * D_NA])
    o = 3 * D_NA
    f = z[..., o:o + D_FOURIER]
    o += D_FOURIER
    a = z[..., o:o + D_CONV]
    g = z[..., o + D_CONV:o + 2 * D_CONV]
    return q, k, v, f, a, g


def fourier_mix(f):
    b_, n, _ = f.shape
    fg = f.astype(jnp.float32).reshape(b_, n, N_FOURIER_GROUPS, FOURIER_GROUP)
    out = jnp.fft.fft2(fg, axes=(1, 3), norm="ortho").real
    return out.reshape(b_, n, D_FOURIER).astype(f.dtype)


def conv_module(a, g, w_dw, b_dw, ln_g, ln_b):
    u = a * jax.nn.sigmoid(g)
    y = lax.conv_general_dilated(
        u, w_dw[:, None, :], window_strides=(1,),
        padding=[(CONV_WIDTH // 2, CONV_WIDTH // 2)],
        dimension_numbers=("NWC", "WIO", "NWC"),
        feature_group_count=D_CONV)
    y = y + b_dw
    return jax.nn.silu(layer_norm(y, ln_g, ln_b))


def neighbourhood_attention(q, k, v, k_ctx, v_ctx, rpb):
    b_, n, h_, dh = q.shape
    rows = n // GRID_W
    wr = min(WIN_R, rows)
    scale = dh ** -0.5

    def grid(t):
        return t.reshape(b_, rows, GRID_W, h_, dh).transpose(0, 3, 1, 2, 4)

    qg, kg, vg = grid(q), grid(k), grid(v)
    r = jnp.arange(rows)
    r0 = jnp.clip(r - wr // 2, 0, rows - wr)
    row_idx = r0[:, None] + jnp.arange(wr)[None, :]
    n_band = wr * GRID_W
    k_band = kg[:, :, row_idx].reshape(b_, h_, rows, n_band, dh)
    v_band = vg[:, :, row_idx].reshape(b_, h_, rows, n_band, dh)

    cq = jnp.arange(GRID_W)
    c0 = jnp.clip(cq - WIN_C // 2, 0, GRID_W - WIN_C)
    col_ok = (cq[None, :] >= c0[:, None]) & (cq[None, :] < c0[:, None] + WIN_C)
    dc_i = jnp.clip(cq[None, :] - cq[:, None], -(WIN_C - 1), WIN_C - 1) + WIN_C - 1
    dr_i = row_idx - r[:, None] + WIN_R - 1
    bias = rpb.astype(jnp.float32)[:, dr_i[:, :, None, None], dc_i[None, None, :, :]]
    bias = jnp.where(col_ok[None, None, None], bias, NEG_INF)
    bias = bias.transpose(0, 1, 3, 2, 4).reshape(h_, rows, GRID_W, n_band)

    s_lat = jnp.einsum('bhrqd,bhrkd->bhrqk', qg, k_band,
                       preferred_element_type=jnp.float32) * scale + bias[None]
    s_ctx = jnp.einsum('bhrqd,bchd->bhrqc', qg, k_ctx,
                       preferred_element_type=jnp.float32) * scale
    p = jax.nn.softmax(jnp.concatenate([s_lat, s_ctx], axis=-1), axis=-1)
    p_lat = p[..., :n_band].astype(v.dtype)
    p_ctx = p[..., n_band:].astype(v.dtype)
    o = (jnp.einsum('bhrqk,bhrkd->bhrqd', p_lat, v_band)
         + jnp.einsum('bhrqc,bchd->bhrqd', p_ctx, v_ctx))
    return o.transpose(0, 2, 3, 1, 4).reshape(b_, n, h_ * dh)


def context_attention(q, k, v):
    b_, n, h_, dh = q.shape
    s = jnp.einsum('bqhd,bkhd->bhqk', q, k, preferred_element_type=jnp.float32) * (dh ** -0.5)
    p = jax.nn.softmax(s, axis=-1).astype(v.dtype)
    return jnp.einsum('bhqk,bkhd->bqhd', p, v).reshape(b_, n, h_ * dh)


def merge_branches(h, y_na, y_f, y_c, w_gate, b_gate, w_b_na, w_b_f, w_b_c, w_o):
    b_, n, _ = h.shape
    gates = jax.nn.sigmoid(h @ w_gate + b_gate).reshape(b_, n, N_BRANCH, D_MODEL)
    m = (gates[:, :, 0] * (y_na @ w_b_na)
         + gates[:, :, 1] * (y_f @ w_b_f)
         + gates[:, :, 2] * (y_c @ w_b_c))
    return m @ w_o


def setup_inputs(seed: int = 0) -> dict:
    key = jax.random.key(seed)
    ks = jax.random.split(key, 32)
    nrm = lambda k, shape, s: jax.random.normal(k, shape, jnp.float32) * s
    L = DEPTH
    return {
        "x": nrm(ks[0], (BATCH, SEQ, D_MODEL), 1.0),
        "c": nrm(ks[1], (BATCH, D_MODEL), 1.0),
        "ctx": nrm(ks[2], (BATCH, CTX_LEN, D_MODEL), 1.0),
        "c_ctx": nrm(ks[3], (D_MODEL,), 1.0),
        "w_ada": nrm(ks[4], (L, D_MODEL, N_MOD * D_MODEL), 0.5 * D_MODEL ** -0.5),
        "b_ada": nrm(ks[5], (L, N_MOD * D_MODEL), 0.02),
        "ffn1_w1": nrm(ks[6], (L, D_MODEL, D_FF), D_MODEL ** -0.5),
        "ffn1_w3": nrm(ks[7], (L, D_MODEL, D_FF), D_MODEL ** -0.5),
        "ffn1_w2": nrm(ks[8], (L, D_FF, D_MODEL), BETA * D_FF ** -0.5),
        "ffn2_w1": nrm(ks[9], (L, D_MODEL, D_FF), D_MODEL ** -0.5),
        "ffn2_w3": nrm(ks[10], (L, D_MODEL, D_FF), D_MODEL ** -0.5),
        "ffn2_w2": nrm(ks[11], (L, D_FF, D_MODEL), BETA * D_FF ** -0.5),
        "w_in": nrm(ks[12], (L, D_MODEL, D_IN), D_MODEL ** -0.5),
        "w_dw": nrm(ks[13], (L, CONV_WIDTH, D_CONV), CONV_WIDTH ** -0.5),
        "b_dw": nrm(ks[14], (L, D_CONV), 0.02),
        "conv_ln_g": 1.0 + nrm(ks[15], (L, D_CONV), 0.02),
        "conv_ln_b": nrm(ks[16], (L, D_CONV), 0.02),
        "rpb": nrm(ks[17], (L, N_HEADS_NA, 2 * WIN_R - 1, 2 * WIN_C - 1), 0.1),
        "w_gate": nrm(ks[18], (L, D_MODEL, N_BRANCH * D_MODEL), D_MODEL ** -0.5),
        "b_gate": nrm(ks[19], (L, N_BRANCH * D_MODEL), 0.02),
        "w_b_na": nrm(ks[20], (L, D_NA, D_MODEL), BETA * D_NA ** -0.5),
        "w_b_f": nrm(ks[21], (L, D_FOURIER, D_MODEL), BETA * D_FOURIER ** -0.5),
        "w_b_c": nrm(ks[22], (L, D_CONV, D_MODEL), BETA * D_CONV ** -0.5),
        "w_o": nrm(ks[23], (L, D_MODEL, D_MODEL), BETA * D_MODEL ** -0.5),
        "ln_g": 1.0 + nrm(ks[24], (L, 3, D_MODEL), 0.02),
        "ln_b": nrm(ks[25], (L, 3, D_MODEL), 0.02),
    }


def reference(x, c, ctx, c_ctx, w_ada, b_ada, ffn1_w1, ffn1_w3, ffn1_w2, ffn2_w1, ffn2_w3, ffn2_w2,
              w_in, w_dw, b_dw, conv_ln_g, conv_ln_b, rpb, w_gate, b_gate, w_b_na, w_b_f, w_b_c,
              w_o, ln_g, ln_b):
    b_ = x.shape[0]
    xc = ctx
    for l in range(DEPTH):
        last = l == DEPTH - 1
        mod = (jax.nn.silu(c) @ w_ada[l] + b_ada[l]).reshape(b_, 1, N_MOD, D_MODEL)
        modc = (jax.nn.silu(c_ctx) @ w_ada[l] + b_ada[l]).reshape(1, 1, N_MOD, D_MODEL)

        x = ffn_sublayer(x, mod, 0, ffn1_w1[l], ffn1_w3[l], ffn1_w2[l], ln_g[l, 0], ln_b[l, 0])
        xc = ffn_sublayer(xc, modc, 0, ffn1_w1[l], ffn1_w3[l], ffn1_w2[l], ln_g[l, 0], ln_b[l, 0])

        h = modulate(x, mod[:, :, 3], mod[:, :, 4])
        hc = modulate(xc, modc[:, :, 3], modc[:, :, 4])
        q, k, v, f, a, g = split_in(h @ w_in[l])
        if last:
            kv_c = (hc @ w_in[l][:, D_NA:3 * D_NA]).reshape(b_, -1, 2, N_HEADS_NA, HEAD_DIM)
            kc, vc = kv_c[:, :, 0], kv_c[:, :, 1]
        else:
            qc, kc, vc, fc, ac, gc = split_in(hc @ w_in[l])

        y_na = neighbourhood_attention(q, k, v, kc, vc, rpb[l])
        y_f = fourier_mix(f)
        y_c = conv_module(a, g, w_dw[l], b_dw[l], conv_ln_g[l], conv_ln_b[l])
        out = merge_branches(h, y_na, y_f, y_c, w_gate[l], b_gate[l], w_b_na[l], w_b_f[l], w_b_c[l], w_o[l])
        x = layer_norm(ALPHA * x + mod[:, :, 5] * out, ln_g[l, 1], ln_b[l, 1])

        if not last:
            yc_na = context_attention(qc, kc, vc)
            yc_f = fourier_mix(fc)
            yc_c = conv_module(ac, gc, w_dw[l], b_dw[l], conv_ln_g[l], conv_ln_b[l])
            outc = merge_branches(hc, yc_na, yc_f, yc_c, w_gate[l], b_gate[l], w_b_na[l], w_b_f[l], w_b_c[l], w_o[l])
            xc = layer_norm(ALPHA * xc + modc[:, :, 5] * outc, ln_g[l, 1], ln_b[l, 1])

        x = ffn_sublayer(x, mod, 1, ffn2_w1[l], ffn2_w3[l], ffn2_w2[l], ln_g[l, 2], ln_b[l, 2])
        if not last:
            xc = ffn_sublayer(xc, modc, 1, ffn2_w1[l], ffn2_w3[l], ffn2_w2[l], ln_g[l, 2], ln_b[l, 2])
    return x
```

```python
import functools

import numpy as np
import jax
import jax.numpy as jnp
from jax import lax
from jax.experimental import pallas as pl
from jax.experimental.pallas import tpu as pltpu

F32 = jnp.float32
BF16 = jnp.bfloat16

D_MODEL = 2048
DEPTH = 2
GRID_W = 64
N_HEADS = 8
HEAD_DIM = 128
D_NA = N_HEADS * HEAD_DIM
WIN_R = 8
WIN_C = 16
FOURIER_GROUP = 128
D_FOURIER = 512
D_CONV = 512
CONV_WIDTH = 31
D_FF = 5632
D_IN = 3 * D_NA + D_FOURIER + 2 * D_CONV
N_MOD_USED = 6
ALPHA = (2 * DEPTH) ** 0.25
LN_EPS = 1e-5
NEG_INF = -1e30

COL_F = 3 * D_NA // D_FOURIER
COL_A = COL_F + 1
COL_G = COL_F + 2

Q_ROWS = 4
KW_ROWS = 12
CONV_HALO = 16

VMEM_LIMIT_V7X = 58 * 1024 * 1024


def _cparams(sem):
    return pltpu.CompilerParams(dimension_semantics=sem, vmem_limit_bytes=VMEM_LIMIT_V7X)


def _layer_norm(r, g, b):
    mu = jnp.mean(r, axis=-1, keepdims=True)
    d = r - mu
    var = jnp.mean(d * d, axis=-1, keepdims=True)
    return d * lax.rsqrt(var + LN_EPS) * g + b


def _dot(a, b):
    return jnp.dot(a, b, preferred_element_type=F32)


def _dot_nt(a, b):
    return lax.dot_general(a, b, (((1,), (1,)), ((), ())), preferred_element_type=F32)


def _ada_kernel(c_ref, w_ref, b_ref, o_ref):
    c = c_ref[...]
    s = (c * jax.nn.sigmoid(c)).astype(BF16)
    o_ref[0] = _dot(s, w_ref[0].astype(BF16)) + b_ref[0]


def _ada_mods(c_rows, w_ada, b_ada, tn=1024):
    n_l, d, _ = w_ada.shape
    n_out = N_MOD_USED * d
    return pl.pallas_call(
        _ada_kernel,
        grid=(n_l, n_out // tn),
        in_specs=[
            pl.BlockSpec((8, d), lambda l, j: (0, 0)),
            pl.BlockSpec((1, d, tn), lambda l, j: (l, 0, j)),
            pl.BlockSpec((1, 1, tn), lambda l, j: (l, 0, j)),
        ],
        out_specs=pl.BlockSpec((1, 8, tn), lambda l, j: (l, 0, j)),
        out_shape=jax.ShapeDtypeStruct((n_l, 8, n_out), F32),
        compiler_params=_cparams(("parallel", "arbitrary")),
        name="ada_mods",
    )(c_rows, w_ada, b_ada.reshape(n_l, 1, -1))


def _residual_layer_norm(pre_norm, g_ref, b_ref, o_ref, cw):
    d = o_ref.shape[1]
    chunks = [slice(c * cw, (c + 1) * cw) for c in range(d // cw)]
    total = 0.0
    for cs in chunks:
        r = pre_norm(cs)
        o_ref[:, cs] = r
        total = total + jnp.sum(r, axis=-1, keepdims=True)
    mu = total / d
    sq = 0.0
    for cs in chunks:
        dev = o_ref[:, cs] - mu
        sq = sq + jnp.sum(dev * dev, axis=-1, keepdims=True)
    inv = lax.rsqrt(sq / d + LN_EPS)
    for cs in chunks:
        o_ref[:, cs] = (o_ref[:, cs] - mu) * inv * g_ref[:, cs] + b_ref[:, cs]


def _ffn_kernel(z_ref, mod_ref, w1_ref, w3_ref, w2_ref, g_ref, b_ref, o_ref, h_ref, *, j, cw):
    k = pl.program_id(1)
    d = o_ref.shape[1]
    chunks = [slice(c * cw, (c + 1) * cw) for c in range(d // cw)]

    @pl.when(k == 0)
    def _():
        for cs in chunks:
            shift = mod_ref[3 * j:3 * j + 1, cs]
            scale = mod_ref[3 * j + 1:3 * j + 2, cs]
            h_ref[:, cs] = (z_ref[:, cs] * (1.0 + scale) + shift).astype(BF16)
            o_ref[:, cs] = jnp.zeros((o_ref.shape[0], cw), F32)

    h = h_ref[...]
    a = _dot(h, w1_ref[...])
    b = _dot(h, w3_ref[...])
    u = (a * jax.nn.sigmoid(a) * b).astype(BF16)

    for cs in chunks:
        o_ref[:, cs] += _dot(u, w2_ref[:, cs])

    @pl.when(k == pl.num_programs(1) - 1)
    def _():
        def pre_norm(cs):
            gate = mod_ref[3 * j + 2:3 * j + 3, cs]
            return ALPHA * z_ref[:, cs] + (0.5 * gate) * o_ref[:, cs]

        _residual_layer_norm(pre_norm, g_ref, b_ref, o_ref, cw)


def _ffn(z, mod, j, w1, w3, w2, ln_g, ln_b, tm, tf, cw=512):
    n, d = z.shape
    dff = w1.shape[1]
    return pl.pallas_call(
        functools.partial(_ffn_kernel, j=j, cw=cw),
        grid=(n // tm, dff // tf),
        in_specs=[
            pl.BlockSpec((tm, d), lambda i, k: (i, 0)),
            pl.BlockSpec(mod.shape, lambda i, k: (0, 0)),
            pl.BlockSpec((d, tf), lambda i, k: (0, k)),
            pl.BlockSpec((d, tf), lambda i, k: (0, k)),
            pl.BlockSpec((tf, d), lambda i, k: (k, 0)),
            pl.BlockSpec((1, d), lambda i, k: (0, 0)),
            pl.BlockSpec((1, d), lambda i, k: (0, 0)),
        ],
        out_specs=pl.BlockSpec((tm, d), lambda i, k: (i, 0)),
        out_shape=jax.ShapeDtypeStruct((n, d), F32),
        scratch_shapes=[pltpu.VMEM((tm, d), BF16)],
        compiler_params=_cparams(("parallel", "arbitrary")),
        name="ffn",
    )(z, mod, w1, w3, w2, ln_g.reshape(1, d), ln_b.reshape(1, d))


def _inproj_kernel(x_ref, mod_ref, w_ref, o_ref, h_ref, *, cw):
    @pl.when(pl.program_id(1) == 0)
    def _():
        for c in range(x_ref.shape[1] // cw):
            cs = slice(c * cw, (c + 1) * cw)
            h_ref[:, cs] = (x_ref[:, cs] * (1.0 + mod_ref[4:5, cs]) + mod_ref[3:4, cs]).astype(BF16)

    o_ref[...] = _dot(h_ref[...], w_ref[...]).astype(BF16)


def _inproj(x, mod, w_in, tm, tn, cw=512):
    n, d = x.shape
    d_in = w_in.shape[1]
    return pl.pallas_call(
        functools.partial(_inproj_kernel, cw=cw),
        grid=(n // tm, d_in // tn),
        in_specs=[
            pl.BlockSpec((tm, d), lambda i, c: (i, 0)),
            pl.BlockSpec(mod.shape, lambda i, c: (0, 0)),
            pl.BlockSpec((d, tn), lambda i, c: (0, c)),
        ],
        out_specs=pl.BlockSpec((tm, tn), lambda i, c: (i, c)),
        out_shape=jax.ShapeDtypeStruct((n, d_in), BF16),
        scratch_shapes=[pltpu.VMEM((tm, d), BF16)],
        compiler_params=_cparams(("parallel", "arbitrary")),
        name="inproj",
    )(x, mod, w_in)


def _na_geometry(rows):
    assert rows % Q_ROWS == 0 and rows >= KW_ROWS and KW_ROWS % 2 == 0
    wr = min(WIN_R, rows)
    n_rb = rows // Q_ROWS
    q0 = np.arange(n_rb) * Q_ROWS
    kstart = np.clip(q0 - WIN_R // 2, 0, rows - KW_ROWS)
    offs = kstart - q0
    uniq = sorted(set(offs.tolist()), reverse=True)
    cfg = np.array([uniq.index(o) for o in offs.tolist()], np.int32)
    dr_cfg, ok_cfg = [], []
    for off in uniq:
        rb = int(np.nonzero(offs == off)[0][0])
        rq = q0[rb] + np.arange(Q_ROWS)
        rk = kstart[rb] + np.arange(KW_ROWS)
        r0 = np.clip(rq - wr // 2, 0, rows - wr)
        ok_cfg.append((rk[None, :] >= r0[:, None]) & (rk[None, :] < r0[:, None] + wr))
        dr_cfg.append(rk[None, :] - rq[:, None] + WIN_R - 1)
    return n_rb, kstart.astype(np.int32), cfg, np.stack(dr_cfg), np.stack(ok_cfg)


def _na_build_bias(h, rpb_ref, t_ref, bias_ref, dr_cfg, ok_cfg):
    n_dr, n_dc = 2 * WIN_R - 1, 2 * WIN_C - 1
    cq = lax.broadcasted_iota(jnp.int32, (GRID_W, 2 * GRID_W), 0)
    lane = lax.broadcasted_iota(jnp.int32, (GRID_W, 2 * GRID_W), 1)
    ck = lane & (GRID_W - 1)
    c0 = jnp.clip(cq - WIN_C // 2, 0, GRID_W - WIN_C)
    col_ok = (ck >= c0) & (ck < c0 + WIN_C)
    dc = jnp.clip(ck - cq, -(WIN_C - 1), WIN_C - 1) + WIN_C - 1
    base = h * (n_dr * n_dc)
    for dr in range(n_dr):
        acc = jnp.zeros((GRID_W, 2 * GRID_W), F32)
        for j in range(n_dc):
            acc = jnp.where(dc == j, rpb_ref[base + dr * n_dc + j], acc)
        t_ref[dr] = jnp.where(col_ok, acc, NEG_INF)
    left = lane < GRID_W
    neg = jnp.full((GRID_W, 2 * GRID_W), NEG_INF, F32)
    for c in range(dr_cfg.shape[0]):
        for rq in range(Q_ROWS):
            for rk in range(0, KW_ROWS, 2):
                even = t_ref[int(dr_cfg[c, rq, rk])] if ok_cfg[c, rq, rk] else neg
                odd = t_ref[int(dr_cfg[c, rq, rk + 1])] if ok_cfg[c, rq, rk + 1] else neg
                bias_ref[c, rq * GRID_W:(rq + 1) * GRID_W, rk * GRID_W:(rk + 2) * GRID_W] = jnp.where(left, even, odd)


def _na_kernel(cfg_ref, ks_ref, rpb_ref, q_ref, k_ref, v_ref, kc_ref, vc_ref, o_ref, t_ref, bias_ref,
               *, dr_cfg, ok_cfg):
    h = pl.program_id(0)
    rb = pl.program_id(1)

    @pl.when(rb == 0)
    def _():
        _na_build_bias(h, rpb_ref, t_ref, bias_ref, dr_cfg, ok_cfg)

    k0 = pl.multiple_of(ks_ref[rb] * GRID_W, GRID_W)
    scale = HEAD_DIM ** -0.5
    q = q_ref[...]
    kwin = k_ref[pl.ds(k0, KW_ROWS * GRID_W), :]
    vwin = v_ref[pl.ds(k0, KW_ROWS * GRID_W), :]
    s_lat = _dot_nt(q, kwin) * scale + bias_ref[cfg_ref[rb]]
    s_ctx = _dot_nt(q, kc_ref[...]) * scale
    m = jnp.maximum(jnp.max(s_lat, axis=-1, keepdims=True), jnp.max(s_ctx, axis=-1, keepdims=True))
    p_lat = jnp.exp(s_lat - m)
    p_ctx = jnp.exp(s_ctx - m)
    denom = jnp.sum(p_lat, axis=-1, keepdims=True) + jnp.sum(p_ctx, axis=-1, keepdims=True)
    o = _dot(p_lat.astype(BF16), vwin) + _dot(p_ctx.astype(BF16), vc_ref[...])
    o_ref[...] = (o / denom).astype(BF16)


def _na_attention(zin, zc, rpb_l):
    n = zin.shape[0]
    n_ctx = zc.shape[0]
    rows = n // GRID_W
    n_rb, kstart, cfg, dr_cfg, ok_cfg = _na_geometry(rows)
    tq = Q_ROWS * GRID_W
    hb = D_NA // HEAD_DIM
    grid_spec = pltpu.PrefetchScalarGridSpec(
        num_scalar_prefetch=3,
        grid=(N_HEADS, n_rb),
        in_specs=[
            pl.BlockSpec((tq, HEAD_DIM), lambda h, r, *_: (r, h)),
            pl.BlockSpec((n, HEAD_DIM), lambda h, r, *_: (0, hb + h)),
            pl.BlockSpec((n, HEAD_DIM), lambda h, r, *_: (0, 2 * hb + h)),
            pl.BlockSpec((n_ctx, HEAD_DIM), lambda h, r, *_: (0, hb + h)),
            pl.BlockSpec((n_ctx, HEAD_DIM), lambda h, r, *_: (0, 2 * hb + h)),
        ],
        out_specs=pl.BlockSpec((tq, HEAD_DIM), lambda h, r, *_: (r, h)),
        scratch_shapes=[
            pltpu.VMEM((2 * WIN_R - 1, GRID_W, 2 * GRID_W), F32),
            pltpu.VMEM((dr_cfg.shape[0], tq, KW_ROWS * GRID_W), F32),
        ],
    )
    return pl.pallas_call(
        functools.partial(_na_kernel, dr_cfg=dr_cfg, ok_cfg=ok_cfg),
        grid_spec=grid_spec,
        out_shape=jax.ShapeDtypeStruct((n, D_NA), BF16),
        compiler_params=_cparams(("arbitrary", "arbitrary")),
        name="na_attention",
    )(jnp.asarray(cfg), jnp.asarray(kstart), rpb_l.astype(F32).reshape(-1), zin, zin, zin, zc, zc)


def _ctx_attn_kernel(q_ref, k_ref, v_ref, o_ref):
    s = _dot_nt(q_ref[...], k_ref[...]) * (HEAD_DIM ** -0.5)
    p = jnp.exp(s - jnp.max(s, axis=-1, keepdims=True))
    o = _dot(p.astype(BF16), v_ref[...])
    o_ref[...] = (o / jnp.sum(p, axis=-1, keepdims=True)).astype(BF16)


def _ctx_attention(zc):
    n_ctx = zc.shape[0]
    hb = D_NA // HEAD_DIM
    return pl.pallas_call(
        _ctx_attn_kernel,
        grid=(N_HEADS,),
        in_specs=[
            pl.BlockSpec((n_ctx, HEAD_DIM), lambda h: (0, h)),
            pl.BlockSpec((n_ctx, HEAD_DIM), lambda h: (0, hb + h)),
            pl.BlockSpec((n_ctx, HEAD_DIM), lambda h: (0, 2 * hb + h)),
        ],
        out_specs=pl.BlockSpec((n_ctx, HEAD_DIM), lambda h: (0, h)),
        out_shape=jax.ShapeDtypeStruct((n_ctx, D_NA), BF16),
        compiler_params=_cparams(("parallel",)),
        name="ctx_attention",
    )(zc, zc, zc)


def _pow2_near(v):
    return 2.0 ** np.round(np.log2(v))


def _bf16_const(a):
    return jnp.asarray(a, F32).astype(BF16)


def _dft_consts(n1, n2):
    k = FOURIER_GROUP
    n_g = D_FOURIER // k
    n = n1 * n2
    sa, sb = 1.0 / _pow2_near(np.sqrt(k)), 1.0 / _pow2_near(np.sqrt(n2))
    jj = np.arange(k)
    ang = 2 * np.pi * ((jj[:, None] * jj[None, :]) % k) / k
    eye = np.eye(n_g)
    a_mat = np.concatenate([np.kron(eye, np.cos(ang)), -np.kron(eye, np.sin(ang))], axis=1) * sa
    ii = np.arange(n2)
    ang2 = 2 * np.pi * ((ii[:, None] * ii[None, :]) % n2) / n2
    c2, s2 = np.cos(ang2), np.sin(ang2)
    b_mat = np.block([[c2, s2], [-s2, c2]]) * sb
    m2 = np.arange(n2)[:, None, None]
    m1 = np.arange(n1)[None, :, None]
    nn1 = np.arange(n1)[None, None, :]
    ang3 = 2 * np.pi * ((nn1 * (n2 * m1 + m2)) % n) / n
    c_mat = np.concatenate([np.cos(ang3), np.sin(ang3)], axis=2)
    out_scale = float(1.0 / (np.sqrt(k * n) * sa * sb))
    return _bf16_const(a_mat), _bf16_const(b_mat), _bf16_const(c_mat), out_scale


def _fa_kernel(f_ref, a_ref, o_ref):
    z = _dot(f_ref[...], a_ref[...])
    o_ref[0] = z[:, :D_FOURIER].astype(BF16)
    o_ref[1] = z[:, D_FOURIER:].astype(BF16)


def _fb_kernel(z_ref, b_ref, o_ref):
    for j in range(z_ref.shape[1]):
        o_ref[:, j, :] = _dot(b_ref[...], z_ref[:, j, :]).astype(BF16)


def _fc_kernel(ur_ref, ui_ref, c_ref, o_ref, *, n1, mb, out_scale):
    for j in range(mb):
        cm = c_ref[j]
        res = _dot(cm[:, :n1], ur_ref[j]) + _dot(cm[:, n1:], ui_ref[j])
        o_ref[:, j, :] = res * out_scale


def _fourier_ctx_kernel(z_ref, c_ref, o_ref, *, n, out_scale):
    o_ref[...] = (_dot(c_ref[:, :n], z_ref[0]) + _dot(c_ref[:, n:], z_ref[1])) * out_scale


def _fourier_stage_a(zin, a_mat, tm):
    n = zin.shape[0]
    return pl.pallas_call(
        _fa_kernel,
        grid=(n // tm,),
        in_specs=[
            pl.BlockSpec((tm, D_FOURIER), lambda i: (i, COL_F)),
            pl.BlockSpec(a_mat.shape, lambda i: (0, 0)),
        ],
        out_specs=pl.BlockSpec((2, tm, D_FOURIER), lambda i: (0, i, 0)),
        out_shape=jax.ShapeDtypeStruct((2, n, D_FOURIER), BF16),
        compiler_params=_cparams(("parallel",)),
        name="fourier_a",
    )(zin, a_mat)


def _fourier_latent(zin, n1, n2, jt=16, mb=8):
    n = zin.shape[0]
    assert n == n1 * n2
    a_mat, b_mat, c_mat, out_scale = _dft_consts(n1, n2)
    zri = _fourier_stage_a(zin, a_mat, tm=min(n, 1024)).reshape(2 * n2, n1, D_FOURIER)
    jt = min(jt, n1)
    u = pl.pallas_call(
        _fb_kernel,
        grid=(n1 // jt,),
        in_specs=[
            pl.BlockSpec((2 * n2, jt, D_FOURIER), lambda c: (0, c, 0)),
            pl.BlockSpec(b_mat.shape, lambda c: (0, 0)),
        ],
        out_specs=pl.BlockSpec((2 * n2, jt, D_FOURIER), lambda c: (0, c, 0)),
        out_shape=jax.ShapeDtypeStruct((2 * n2, n1, D_FOURIER), BF16),
        compiler_params=_cparams(("parallel",)),
        name="fourier_b",
    )(zri, b_mat)
    nb = n2 // mb
    out = pl.pallas_call(
        functools.partial(_fc_kernel, n1=n1, mb=mb, out_scale=out_scale),
        grid=(nb,),
        in_specs=[
            pl.BlockSpec((mb, n1, D_FOURIER), lambda b: (b, 0, 0)),
            pl.BlockSpec((mb, n1, D_FOURIER), lambda b: (nb + b, 0, 0)),
            pl.BlockSpec((mb, n1, 2 * n1), lambda b: (b, 0, 0)),
        ],
        out_specs=pl.BlockSpec((n1, mb, D_FOURIER), lambda b: (0, b, 0)),
        out_shape=jax.ShapeDtypeStruct((n1, n2, D_FOURIER), F32),
        compiler_params=_cparams(("parallel",)),
        name="fourier_c",
    )(u, u, c_mat)
    return out.reshape(n, D_FOURIER)


def _fourier_ctx(zc):
    n = zc.shape[0]
    a_mat, _, _, out_scale = _dft_consts(n, 1)
    ii = np.arange(n)
    ang = 2 * np.pi * ((ii[:, None] * ii[None, :]) % n) / n
    c_mat = _bf16_const(np.concatenate([np.cos(ang), np.sin(ang)], axis=1))
    zri = _fourier_stage_a(zc, a_mat, tm=n)
    return pl.pallas_call(
        functools.partial(_fourier_ctx_kernel, n=n, out_scale=out_scale),
        grid=(1,),
        in_specs=[
            pl.BlockSpec((2, n, D_FOURIER), lambda i: (0, 0, 0)),
            pl.BlockSpec(c_mat.shape, lambda i: (0, 0)),
        ],
        out_specs=pl.BlockSpec((n, D_FOURIER), lambda i: (0, 0)),
        out_shape=jax.ShapeDtypeStruct((n, D_FOURIER), F32),
        compiler_params=_cparams(("arbitrary",)),
        name="fourier_ctx",
    )(zri, c_mat)


def _conv_kernel(ap_ref, ac_ref, an_ref, gp_ref, gc_ref, gn_ref, w_ref, b_ref, lg_ref, lb_ref, o_ref, win_ref,
                 *, tm, rs):
    i = pl.program_id(0)

    def glu(a_ref, g_ref):
        return a_ref[...].astype(F32) * jax.nn.sigmoid(g_ref[...].astype(F32))

    win_ref[0:CONV_HALO, :] = jnp.where(i == 0, 0.0, glu(ap_ref, gp_ref))
    win_ref[CONV_HALO:CONV_HALO + tm, :] = glu(ac_ref, gc_ref)
    win_ref[CONV_HALO + tm:, :] = jnp.where(i == pl.num_programs(0) - 1, 0.0, glu(an_ref, gn_ref))

    first = CONV_HALO - CONV_WIDTH // 2
    lanes = 128
    for r in range(tm // rs):
        parts = []
        for c in range(D_CONV // lanes):
            cs = slice(c * lanes, (c + 1) * lanes)
            acc = jnp.zeros((rs, lanes), F32)
            for t in range(CONV_WIDTH):
                acc = acc + win_ref[r * rs + first + t:r * rs + first + t + rs, cs] * w_ref[t:t + 1, cs]
            parts.append(acc)
        y = jnp.concatenate(parts, axis=-1) + b_ref[...]
        y = _layer_norm(y, lg_ref[...], lb_ref[...])
        o_ref[r * rs:(r + 1) * rs, :] = (y * jax.nn.sigmoid(y)).astype(BF16)


def _conv_module(zin, w_dw, b_dw, ln_g, ln_b, tm, rs=32):
    n = zin.shape[0]
    nt = n // tm
    hb = tm // CONV_HALO
    n_hb = n // CONV_HALO
    prev_map = lambda col: (lambda i: (jnp.maximum(i * hb - 1, 0), col))
    next_map = lambda col: (lambda i: (jnp.minimum((i + 1) * hb, n_hb - 1), col))
    cur_map = lambda col: (lambda i: (i, col))
    vec = lambda v: v.reshape(1, D_CONV).astype(F32)
    return pl.pallas_call(
        functools.partial(_conv_kernel, tm=tm, rs=rs),
        grid=(nt,),
        in_specs=[
            pl.BlockSpec((CONV_HALO, D_CONV), prev_map(COL_A)),
            pl.BlockSpec((tm, D_CONV), cur_map(COL_A)),
            pl.BlockSpec((CONV_HALO, D_CONV), next_map(COL_A)),
            pl.BlockSpec((CONV_HALO, D_CONV), prev_map(COL_G)),
            pl.BlockSpec((tm, D_CONV), cur_map(COL_G)),
            pl.BlockSpec((CONV_HALO, D_CONV), next_map(COL_G)),
            pl.BlockSpec((CONV_WIDTH, D_CONV), lambda i: (0, 0)),
            pl.BlockSpec((1, D_CONV), lambda i: (0, 0)),
            pl.BlockSpec((1, D_CONV), lambda i: (0, 0)),
            pl.BlockSpec((1, D_CONV), lambda i: (0, 0)),
        ],
        out_specs=pl.BlockSpec((tm, D_CONV), lambda i: (i, 0)),
        out_shape=jax.ShapeDtypeStruct((n, D_CONV), BF16),
        scratch_shapes=[pltpu.VMEM((tm + 2 * CONV_HALO, D_CONV), F32)],
        compiler_params=_cparams(("parallel",)),
        name="conv_module",
    )(zin, zin, zin, zin, zin, zin, w_dw.astype(F32), vec(b_dw), vec(ln_g), vec(ln_b))


def _merge_kernel(x_ref, mod_ref, yna_ref, yf_ref, yc_ref, wg0_ref, wg1_ref, wg2_ref, bg0_ref, bg1_ref, bg2_ref,
                  wna_ref, wf_ref, wc_ref, wo_ref, g_ref, b_ref, o_ref, h_ref, m_ref, acc_ref, *, nc, cw):
    k = pl.program_id(1)

    @pl.when(k == 0)
    def _():
        for c in range(nc):
            cs = slice(c * cw, (c + 1) * cw)
            h_ref[:, cs] = (x_ref[:, cs] * (1.0 + mod_ref[4:5, cs]) + mod_ref[3:4, cs]).astype(BF16)

    @pl.when(k < nc)
    def _():
        h = h_ref[...]
        g0 = jax.nn.sigmoid(_dot(h, wg0_ref[...]) + bg0_ref[...])
        m = g0 * _dot(yna_ref[...], wna_ref[...])
        g1 = jax.nn.sigmoid(_dot(h, wg1_ref[...]) + bg1_ref[...])
        m = m + g1 * _dot(yf_ref[...].astype(BF16), wf_ref[...])
        g2 = jax.nn.sigmoid(_dot(h, wg2_ref[...]) + bg2_ref[...])
        m = m + g2 * _dot(yc_ref[...], wc_ref[...])
        m_ref[k] = m.astype(BF16)

    @pl.when(k >= nc)
    def _():
        acc = _dot(m_ref[0], wo_ref[0:cw, :])
        for kk in range(1, nc):
            acc = acc + _dot(m_ref[kk], wo_ref[kk * cw:(kk + 1) * cw, :])
        acc_ref[k - nc] = acc

    @pl.when(k == 2 * nc - 1)
    def _():
        def pre_norm(cs):
            return ALPHA * x_ref[:, cs] + mod_ref[5:6, cs] * acc_ref[cs.start // cw]

        _residual_layer_norm(pre_norm, g_ref, b_ref, o_ref, cw)


def _merge(x, mod, y_na, y_f, y_c, w_gate, b_gate, w_b_na, w_b_f, w_b_c, w_o, ln_g, ln_b, tm, cw=512):
    n, d = x.shape
    nc = d // cw
    ga = lambda br: (lambda i, k: (0, br * nc + jnp.minimum(k, nc - 1)))
    pa = lambda i, k: (0, jnp.minimum(k, nc - 1))
    pb = lambda i, k: (0, jnp.maximum(k - nc, 0))
    row = lambda i, k: (i, 0)
    fixed = lambda i, k: (0, 0)
    bg = b_gate.reshape(1, -1)
    return pl.pallas_call(
        functools.partial(_merge_kernel, nc=nc, cw=cw),
        grid=(n // tm, 2 * nc),
        in_specs=[
            pl.BlockSpec((tm, d), row),
            pl.BlockSpec(mod.shape, fixed),
            pl.BlockSpec((tm, D_NA), row),
            pl.BlockSpec((tm, D_FOURIER), row),
            pl.BlockSpec((tm, D_CONV), row),
            pl.BlockSpec((d, cw), ga(0)),
            pl.BlockSpec((d, cw), ga(1)),
            pl.BlockSpec((d, cw), ga(2)),
            pl.BlockSpec((1, cw), ga(0)),
            pl.BlockSpec((1, cw), ga(1)),
            pl.BlockSpec((1, cw), ga(2)),
            pl.BlockSpec((D_NA, cw), pa),
            pl.BlockSpec((D_FOURIER, cw), pa),
            pl.BlockSpec((D_CONV, cw), pa),
            pl.BlockSpec((d, cw), pb),
            pl.BlockSpec((1, d), fixed),
            pl.BlockSpec((1, d), fixed),
        ],
        out_specs=pl.BlockSpec((tm, d), row),
        out_shape=jax.ShapeDtypeStruct((n, d), F32),
        scratch_shapes=[
            pltpu.VMEM((tm, d), BF16),
            pltpu.VMEM((nc, tm, cw), BF16),
            pltpu.VMEM((nc, tm, cw), F32),
        ],
        compiler_params=_cparams(("parallel", "arbitrary")),
        name="merge",
    )(x, mod, y_na, y_f, y_c, w_gate, w_gate, w_gate, bg, bg, bg, w_b_na, w_b_f, w_b_c, w_o,
      ln_g.reshape(1, d), ln_b.reshape(1, d))


def _row_tile(n, target):
    t = min(n, target)
    assert n % t == 0
    return t


def kernel(x, c, ctx, c_ctx, w_ada, b_ada, ffn1_w1, ffn1_w3, ffn1_w2, ffn2_w1, ffn2_w3, ffn2_w2, w_in, w_dw, b_dw, conv_ln_g, conv_ln_b, rpb, w_gate, b_gate, w_b_na, w_b_f, w_b_c, w_o, ln_g, ln_b):
    b_, n, d = x.shape
    assert b_ == 1 and c.shape[0] == 1
    n_ctx = ctx.shape[1]
    depth = w_ada.shape[0]

    c_rows = jnp.zeros((8, d), F32).at[0].set(c[0]).at[1].set(c_ctx)
    mods = _ada_mods(c_rows, w_ada, b_ada)
    bf = lambda w: w.astype(BF16)

    tm = _row_tile(n, 1024)
    tmc = _row_tile(n_ctx, 256)
    xl = x[0]
    xc = ctx[0]
    for l in range(depth):
        last = l == depth - 1
        mod = mods[l, 0].reshape(N_MOD_USED, d)
        modc = mods[l, 1].reshape(N_MOD_USED, d)
        w1a, w3a, w2a = bf(ffn1_w1[l]), bf(ffn1_w3[l]), bf(ffn1_w2[l])
        w1b, w3b, w2b = bf(ffn2_w1[l]), bf(ffn2_w3[l]), bf(ffn2_w2[l])
        w_in_l = bf(w_in[l])
        merge_w = (bf(w_gate[l]), b_gate[l], bf(w_b_na[l]), bf(w_b_f[l]), bf(w_b_c[l]), bf(w_o[l]),
                   ln_g[l, 1], ln_b[l, 1])
        conv_w = (w_dw[l], b_dw[l], conv_ln_g[l], conv_ln_b[l])

        xl = _ffn(xl, mod, 0, w1a, w3a, w2a, ln_g[l, 0], ln_b[l, 0], tm=tm, tf=256)
        xc = _ffn(xc, modc, 0, w1a, w3a, w2a, ln_g[l, 0], ln_b[l, 0], tm=tmc, tf=512)

        zin = _inproj(xl, mod, w_in_l, tm=tm, tn=1536)
        zc = _inproj(xc, modc, w_in_l, tm=tmc, tn=1536)

        y_na = _na_attention(zin, zc, rpb[l])
        y_f = _fourier_latent(zin, n1=n // GRID_W, n2=GRID_W)
        y_c = _conv_module(zin, *conv_w, tm=_row_tile(n, 256))
        xl_new = _merge(xl, mod, y_na, y_f, y_c, *merge_w, tm=_row_tile(n, 512))

        if not last:
            yc_na = _ctx_attention(zc)
            yc_f = _fourier_ctx(zc)
            yc_c = _conv_module(zc, *conv_w, tm=n_ctx)
            xc = _merge(xc, modc, yc_na, yc_f, yc_c, *merge_w, tm=tmc)
        xl = xl_new

        xl = _ffn(xl, mod, 1, w1b, w3b, w2b, ln_g[l, 2], ln_b[l, 2], tm=tm, tf=256)
        if not last:
            xc = _ffn(xc, modc, 1, w1b, w3b, w2b, ln_g[l, 2], ln_b[l, 2], tm=tmc, tf=512)
    return xl[None]
```

```python
import functools

import numpy as np
import jax
import jax.numpy as jnp
from jax import lax
from jax.experimental import pallas as pl
from jax.experimental.pallas import tpu as pltpu

F32 = jnp.float32
BF16 = jnp.bfloat16

D_MODEL = 2048
DEPTH = 2
GRID_W = 64
N_HEADS = 8
HEAD_DIM = 128
D_NA = N_HEADS * HEAD_DIM
WIN_R = 8
WIN_C = 16
FOURIER_GROUP = 128
D_FOURIER = 512
D_CONV = 512
CONV_WIDTH = 31
D_FF = 5632
D_IN = 3 * D_NA + D_FOURIER + 2 * D_CONV
N_MOD_USED = 6
ALPHA = (2 * DEPTH) ** 0.25
LN_EPS = 1e-5
NEG_INF = -1e30

COL_F = 3 * D_NA // D_FOURIER
COL_A = COL_F + 1
COL_G = COL_F + 2

Q_ROWS = 4
KW_ROWS = 12
CONV_HALO = 16
FFN_TF = 256
INPROJ_TN = 1536
MERGE_CW = 512

VMEM_LIMIT_V7X = 60 * 1024 * 1024


def _cparams(sem):
    return pltpu.CompilerParams(dimension_semantics=sem, vmem_limit_bytes=VMEM_LIMIT_V7X)


def _layer_norm(r, g, b):
    mu = jnp.mean(r, axis=-1, keepdims=True)
    d = r - mu
    var = jnp.mean(d * d, axis=-1, keepdims=True)
    return d * lax.rsqrt(var + LN_EPS) * g + b


def _dot(a, b):
    return jnp.dot(a, b, preferred_element_type=F32)


def _dot_nt(a, b):
    return lax.dot_general(a, b, (((1,), (1,)), ((), ())), preferred_element_type=F32)


def _ada_kernel(c_ref, w_ref, b_ref, o_ref):
    c = c_ref[...]
    s = (c * jax.nn.sigmoid(c)).astype(BF16)
    o_ref[0] = _dot(s, w_ref[0].astype(BF16)) + b_ref[0]


def _ada_mods(c_rows, w_ada, b_ada, tn=1024):
    n_l, d, _ = w_ada.shape
    n_out = N_MOD_USED * d
    return pl.pallas_call(
        _ada_kernel,
        grid=(n_l, n_out // tn),
        in_specs=[
            pl.BlockSpec((8, d), lambda l, j: (0, 0)),
            pl.BlockSpec((1, d, tn), lambda l, j: (l, 0, j)),
            pl.BlockSpec((1, 1, tn), lambda l, j: (l, 0, j)),
        ],
        out_specs=pl.BlockSpec((1, 8, tn), lambda l, j: (l, 0, j)),
        out_shape=jax.ShapeDtypeStruct((n_l, 8, n_out), F32),
        compiler_params=_cparams(("parallel", "arbitrary")),
        name="ada_mods",
    )(c_rows, w_ada, b_ada.reshape(n_l, 1, -1))


def _residual_layer_norm(pre_norm, g_ref, b_ref, o_ref, cw, rsub):
    tm, d = o_ref.shape
    chunks = [slice(c * cw, (c + 1) * cw) for c in range(d // cw)]

    def body(r, carry):
        rows = pl.ds(pl.multiple_of(r * rsub, rsub), rsub)
        total = 0.0
        for cs in chunks:
            v = pre_norm(rows, cs)
            o_ref[rows, cs] = v
            total = total + jnp.sum(v, axis=-1, keepdims=True)
        mu = total / d
        sq = 0.0
        for cs in chunks:
            dev = o_ref[rows, cs] - mu
            sq = sq + jnp.sum(dev * dev, axis=-1, keepdims=True)
        inv = lax.rsqrt(sq / d + LN_EPS)
        for cs in chunks:
            o_ref[rows, cs] = (o_ref[rows, cs] - mu) * inv * g_ref[:, cs] + b_ref[:, cs]
        return carry

    lax.fori_loop(0, tm // rsub, body, 0)


def _cast_jobs(src_refs, dst_refs):
    for s_ref, d_ref in zip(src_refs, dst_refs):
        w = d_ref.shape[2]
        for c in range(d_ref.shape[0]):
            d_ref[c] = s_ref[0, :, c * w:(c + 1) * w].astype(BF16)


def _cast_job_specs(src, layer, cb, n_steps, step_of):
    _, r, c = src.shape
    rb = next(v for v in range(16, r + 1, 16) if r % v == 0 and r // v <= n_steps)
    nb = r // rb
    blk = lambda *g: jnp.minimum(step_of(*g), nb - 1)
    in_spec = pl.BlockSpec((1, rb, c), lambda *g: (layer, blk(*g), 0))
    out_spec = pl.BlockSpec((cb, rb, c // cb), lambda *g: (0, blk(*g), 0))
    return in_spec, out_spec, jax.ShapeDtypeStruct((cb, r, c // cb), BF16)


def _blocked_bf16(w, cb):
    r, c = w.shape
    return w.astype(BF16).reshape(r, cb, c // cb).transpose(1, 0, 2)


def _ffn_kernel(*refs, j, cw, rsub, n_jobs):
    z_ref, mod_ref, w1_ref, w3_ref, w2_ref, g_ref, b_ref = refs[:7]
    src_refs = refs[7:7 + n_jobs]
    o_ref = refs[7 + n_jobs]
    dst_refs = refs[8 + n_jobs:8 + 2 * n_jobs]
    h_ref = refs[8 + 2 * n_jobs]
    k = pl.program_id(1)
    d = o_ref.shape[1]
    chunks = [slice(c * cw, (c + 1) * cw) for c in range(d // cw)]

    @pl.when(k == 0)
    def _():
        for cs in chunks:
            shift = mod_ref[3 * j:3 * j + 1, cs]
            scale = mod_ref[3 * j + 1:3 * j + 2, cs]
            h_ref[:, cs] = (z_ref[:, cs] * (1.0 + scale) + shift).astype(BF16)
            o_ref[:, cs] = jnp.zeros((o_ref.shape[0], cw), F32)

    _cast_jobs(src_refs, dst_refs)
    h = h_ref[...]
    a = _dot(h, w1_ref[0])
    b = _dot(h, w3_ref[0])
    u = (a * jax.nn.sigmoid(a) * b).astype(BF16)

    for cs in chunks:
        o_ref[:, cs] += _dot(u, w2_ref[0, :, cs])

    @pl.when(k == pl.num_programs(1) - 1)
    def _():
        def pre_norm(rows, cs):
            gate = mod_ref[3 * j + 2:3 * j + 3, cs]
            return ALPHA * z_ref[rows, cs] + (0.5 * gate) * o_ref[rows, cs]

        _residual_layer_norm(pre_norm, g_ref, b_ref, o_ref, cw, rsub)


def _ffn(z, mod, j, w1, w3, w2, ln_g, ln_b, tm, jobs=(), cw=512):
    n, d = z.shape
    nk, _, tf = w1.shape
    step_of = lambda i, k: i * nk + k
    job_specs = [_cast_job_specs(src, layer, cb, (n // tm) * nk, step_of) for src, layer, cb in jobs]
    outs = pl.pallas_call(
        functools.partial(_ffn_kernel, j=j, cw=cw, rsub=min(tm, 256), n_jobs=len(jobs)),
        grid=(n // tm, nk),
        in_specs=[
            pl.BlockSpec((tm, d), lambda i, k: (i, 0)),
            pl.BlockSpec(mod.shape, lambda i, k: (0, 0)),
            pl.BlockSpec((1, d, tf), lambda i, k: (k, 0, 0)),
            pl.BlockSpec((1, d, tf), lambda i, k: (k, 0, 0)),
            pl.BlockSpec((1, tf, d), lambda i, k: (0, k, 0)),
            pl.BlockSpec((1, d), lambda i, k: (0, 0)),
            pl.BlockSpec((1, d), lambda i, k: (0, 0)),
        ] + [js[0] for js in job_specs],
        out_specs=[pl.BlockSpec((tm, d), lambda i, k: (i, 0))] + [js[1] for js in job_specs],
        out_shape=[jax.ShapeDtypeStruct((n, d), F32)] + [js[2] for js in job_specs],
        scratch_shapes=[pltpu.VMEM((tm, d), BF16)],
        compiler_params=_cparams(("arbitrary", "arbitrary")),
        name="ffn",
    )(z, mod, w1, w3, w2, ln_g.reshape(1, d), ln_b.reshape(1, d), *[src for src, _, _ in jobs])
    return outs[0], list(outs[1:])


def _inproj_kernel(x_ref, mod_ref, w_ref, o_ref, h_ref, *, cw):
    @pl.when(pl.program_id(1) == 0)
    def _():
        for c in range(x_ref.shape[1] // cw):
            cs = slice(c * cw, (c + 1) * cw)
            h_ref[:, cs] = (x_ref[:, cs] * (1.0 + mod_ref[4:5, cs]) + mod_ref[3:4, cs]).astype(BF16)

    o_ref[...] = _dot(h_ref[...], w_ref[0]).astype(BF16)


def _inproj(x, mod, w_in, tm, cw=512):
    n, d = x.shape
    nb, _, tn = w_in.shape
    d_in = nb * tn
    return pl.pallas_call(
        functools.partial(_inproj_kernel, cw=cw),
        grid=(n // tm, nb),
        in_specs=[
            pl.BlockSpec((tm, d), lambda i, c: (i, 0)),
            pl.BlockSpec(mod.shape, lambda i, c: (0, 0)),
            pl.BlockSpec((1, d, tn), lambda i, c: (c, 0, 0)),
        ],
        out_specs=pl.BlockSpec((tm, tn), lambda i, c: (i, c)),
        out_shape=jax.ShapeDtypeStruct((n, d_in), BF16),
        scratch_shapes=[pltpu.VMEM((tm, d), BF16)],
        compiler_params=_cparams(("parallel", "arbitrary")),
        name="inproj",
    )(x, mod, w_in)


def _na_geometry(rows):
    assert rows % Q_ROWS == 0 and rows >= KW_ROWS and KW_ROWS % 2 == 0
    wr = min(WIN_R, rows)
    n_rb = rows // Q_ROWS
    q0 = np.arange(n_rb) * Q_ROWS
    kstart = np.clip(q0 - WIN_R // 2, 0, rows - KW_ROWS)
    offs = kstart - q0
    uniq = sorted(set(offs.tolist()), reverse=True)
    cfg = np.array([uniq.index(o) for o in offs.tolist()], np.int32)
    dr_cfg, ok_cfg = [], []
    for off in uniq:
        rb = int(np.nonzero(offs == off)[0][0])
        rq = q0[rb] + np.arange(Q_ROWS)
        rk = kstart[rb] + np.arange(KW_ROWS)
        r0 = np.clip(rq - wr // 2, 0, rows - wr)
        ok_cfg.append((rk[None, :] >= r0[:, None]) & (rk[None, :] < r0[:, None] + wr))
        dr_cfg.append(rk[None, :] - rq[:, None] + WIN_R - 1)
    return n_rb, kstart.astype(np.int32), cfg, np.stack(dr_cfg), np.stack(ok_cfg)


def _na_build_bias(h, rpb_ref, t_ref, bias_ref, dr_cfg, ok_cfg):
    n_dr, n_dc = 2 * WIN_R - 1, 2 * WIN_C - 1
    cq = lax.broadcasted_iota(jnp.int32, (GRID_W, 2 * GRID_W), 0)
    lane = lax.broadcasted_iota(jnp.int32, (GRID_W, 2 * GRID_W), 1)
    ck = lane & (GRID_W - 1)
    c0 = jnp.clip(cq - WIN_C // 2, 0, GRID_W - WIN_C)
    col_ok = (ck >= c0) & (ck < c0 + WIN_C)
    dc = jnp.clip(ck - cq, -(WIN_C - 1), WIN_C - 1) + WIN_C - 1
    base = h * (n_dr * n_dc)
    for dr in range(n_dr):
        acc = jnp.zeros((GRID_W, 2 * GRID_W), F32)
        for j in range(n_dc):
            acc = jnp.where(dc == j, rpb_ref[base + dr * n_dc + j], acc)
        t_ref[dr] = jnp.where(col_ok, acc, NEG_INF)
    left = lane < GRID_W
    neg = jnp.full((GRID_W, 2 * GRID_W), NEG_INF, F32)
    for c in range(dr_cfg.shape[0]):
        for rq in range(Q_ROWS):
            for rk in range(0, KW_ROWS, 2):
                even = t_ref[int(dr_cfg[c, rq, rk])] if ok_cfg[c, rq, rk] else neg
                odd = t_ref[int(dr_cfg[c, rq, rk + 1])] if ok_cfg[c, rq, rk + 1] else neg
                bias_ref[c, rq * GRID_W:(rq + 1) * GRID_W, rk * GRID_W:(rk + 2) * GRID_W] = jnp.where(left, even, odd)


def _na_kernel(cfg_ref, ks_ref, rpb_ref, q_ref, k_ref, v_ref, kc_ref, vc_ref, o_ref, t_ref, bias_ref,
               *, dr_cfg, ok_cfg):
    h = pl.program_id(0)
    rb = pl.program_id(1)

    @pl.when(rb == 0)
    def _():
        _na_build_bias(h, rpb_ref, t_ref, bias_ref, dr_cfg, ok_cfg)

    k0 = pl.multiple_of(ks_ref[rb] * GRID_W, GRID_W)
    scale = HEAD_DIM ** -0.5
    q = q_ref[...]
    kwin = k_ref[pl.ds(k0, KW_ROWS * GRID_W), :]
    vwin = v_ref[pl.ds(k0, KW_ROWS * GRID_W), :]
    s_lat = _dot_nt(q, kwin) * scale + bias_ref[cfg_ref[rb]]
    s_ctx = _dot_nt(q, kc_ref[...]) * scale
    m = jnp.maximum(jnp.max(s_lat, axis=-1, keepdims=True), jnp.max(s_ctx, axis=-1, keepdims=True))
    p_lat = jnp.exp(s_lat - m)
    p_ctx = jnp.exp(s_ctx - m)
    denom = jnp.sum(p_lat, axis=-1, keepdims=True) + jnp.sum(p_ctx, axis=-1, keepdims=True)
    o = _dot(p_lat.astype(BF16), vwin) + _dot(p_ctx.astype(BF16), vc_ref[...])
    o_ref[...] = (o / denom).astype(BF16)


def _na_attention(zin, zc, rpb_l):
    n = zin.shape[0]
    n_ctx = zc.shape[0]
    rows = n // GRID_W
    n_rb, kstart, cfg, dr_cfg, ok_cfg = _na_geometry(rows)
    tq = Q_ROWS * GRID_W
    hb = D_NA // HEAD_DIM
    grid_spec = pltpu.PrefetchScalarGridSpec(
        num_scalar_prefetch=3,
        grid=(N_HEADS, n_rb),
        in_specs=[
            pl.BlockSpec((tq, HEAD_DIM), lambda h, r, *_: (r, h)),
            pl.BlockSpec((n, HEAD_DIM), lambda h, r, *_: (0, hb + h)),
            pl.BlockSpec((n, HEAD_DIM), lambda h, r, *_: (0, 2 * hb + h)),
            pl.BlockSpec((n_ctx, HEAD_DIM), lambda h, r, *_: (0, hb + h)),
            pl.BlockSpec((n_ctx, HEAD_DIM), lambda h, r, *_: (0, 2 * hb + h)),
        ],
        out_specs=pl.BlockSpec((tq, HEAD_DIM), lambda h, r, *_: (r, h)),
        scratch_shapes=[
            pltpu.VMEM((2 * WIN_R - 1, GRID_W, 2 * GRID_W), F32),
            pltpu.VMEM((dr_cfg.shape[0], tq, KW_ROWS * GRID_W), F32),
        ],
    )
    return pl.pallas_call(
        functools.partial(_na_kernel, dr_cfg=dr_cfg, ok_cfg=ok_cfg),
        grid_spec=grid_spec,
        out_shape=jax.ShapeDtypeStruct((n, D_NA), BF16),
        compiler_params=_cparams(("arbitrary", "arbitrary")),
        name="na_attention",
    )(jnp.asarray(cfg), jnp.asarray(kstart), rpb_l.astype(F32).reshape(-1), zin, zin, zin, zc, zc)


def _ctx_attn_kernel(q_ref, k_ref, v_ref, o_ref):
    s = _dot_nt(q_ref[...], k_ref[...]) * (HEAD_DIM ** -0.5)
    p = jnp.exp(s - jnp.max(s, axis=-1, keepdims=True))
    o = _dot(p.astype(BF16), v_ref[...])
    o_ref[...] = (o / jnp.sum(p, axis=-1, keepdims=True)).astype(BF16)


def _ctx_attention(zc):
    n_ctx = zc.shape[0]
    hb = D_NA // HEAD_DIM
    return pl.pallas_call(
        _ctx_attn_kernel,
        grid=(N_HEADS,),
        in_specs=[
            pl.BlockSpec((n_ctx, HEAD_DIM), lambda h: (0, h)),
            pl.BlockSpec((n_ctx, HEAD_DIM), lambda h: (0, hb + h)),
            pl.BlockSpec((n_ctx, HEAD_DIM), lambda h: (0, 2 * hb + h)),
        ],
        out_specs=pl.BlockSpec((n_ctx, HEAD_DIM), lambda h: (0, h)),
        out_shape=jax.ShapeDtypeStruct((n_ctx, D_NA), BF16),
        compiler_params=_cparams(("parallel",)),
        name="ctx_attention",
    )(zc, zc, zc)


def _pow2_near(v):
    return 2.0 ** np.round(np.log2(v))


def _bf16_const(a):
    return jnp.asarray(a, F32).astype(BF16)


def _dft_consts(n1, n2):
    k = FOURIER_GROUP
    n_g = D_FOURIER // k
    n = n1 * n2
    sa, sb = 1.0 / _pow2_near(np.sqrt(k)), 1.0 / _pow2_near(np.sqrt(n2))
    jj = np.arange(k)
    ang = 2 * np.pi * ((jj[:, None] * jj[None, :]) % k) / k
    eye = np.eye(n_g)
    a_mat = np.concatenate([np.kron(eye, np.cos(ang)), -np.kron(eye, np.sin(ang))], axis=1) * sa
    ii = np.arange(n2)
    ang2 = 2 * np.pi * ((ii[:, None] * ii[None, :]) % n2) / n2
    c2, s2 = np.cos(ang2), np.sin(ang2)
    b_mat = np.block([[c2, s2], [-s2, c2]]) * sb
    m2 = np.arange(n2)[:, None, None]
    m1 = np.arange(n1)[None, :, None]
    nn1 = np.arange(n1)[None, None, :]
    ang3 = 2 * np.pi * ((nn1 * (n2 * m1 + m2)) % n) / n
    c_mat = np.concatenate([np.cos(ang3), np.sin(ang3)], axis=2)
    out_scale = float(1.0 / (np.sqrt(k * n) * sa * sb))
    return _bf16_const(a_mat), _bf16_const(b_mat), _bf16_const(c_mat), out_scale


def _fa_kernel(f_ref, a_ref, o_ref):
    z = _dot(f_ref[...], a_ref[...])
    o_ref[0] = z[:, :D_FOURIER].astype(BF16)
    o_ref[1] = z[:, D_FOURIER:].astype(BF16)


def _fb_kernel(z_ref, b_ref, o_ref):
    for j in range(z_ref.shape[1]):
        o_ref[:, j, :] = _dot(b_ref[...], z_ref[:, j, :]).astype(BF16)


def _fc_kernel(ur_ref, ui_ref, c_ref, o_ref, *, n1, mb, out_scale):
    for j in range(mb):
        cm = c_ref[j]
        res = _dot(cm[:, :n1], ur_ref[j]) + _dot(cm[:, n1:], ui_ref[j])
        o_ref[:, j, :] = res * out_scale


def _fourier_ctx_kernel(z_ref, c_ref, o_ref, *, n, out_scale):
    o_ref[...] = (_dot(c_ref[:, :n], z_ref[0]) + _dot(c_ref[:, n:], z_ref[1])) * out_scale


def _fourier_stage_a(zin, a_mat, tm):
    n = zin.shape[0]
    return pl.pallas_call(
        _fa_kernel,
        grid=(n // tm,),
        in_specs=[
            pl.BlockSpec((tm, D_FOURIER), lambda i: (i, COL_F)),
            pl.BlockSpec(a_mat.shape, lambda i: (0, 0)),
        ],
        out_specs=pl.BlockSpec((2, tm, D_FOURIER), lambda i: (0, i, 0)),
        out_shape=jax.ShapeDtypeStruct((2, n, D_FOURIER), BF16),
        compiler_params=_cparams(("parallel",)),
        name="fourier_a",
    )(zin, a_mat)


def _fourier_latent(zin, n1, n2, jt=16, mb=8):
    n = zin.shape[0]
    assert n == n1 * n2
    a_mat, b_mat, c_mat, out_scale = _dft_consts(n1, n2)
    zri = _fourier_stage_a(zin, a_mat, tm=min(n, 1024)).reshape(2 * n2, n1, D_FOURIER)
    jt = min(jt, n1)
    u = pl.pallas_call(
        _fb_kernel,
        grid=(n1 // jt,),
        in_specs=[
            pl.BlockSpec((2 * n2, jt, D_FOURIER), lambda c: (0, c, 0)),
            pl.BlockSpec(b_mat.shape, lambda c: (0, 0)),
        ],
        out_specs=pl.BlockSpec((2 * n2, jt, D_FOURIER), lambda c: (0, c, 0)),
        out_shape=jax.ShapeDtypeStruct((2 * n2, n1, D_FOURIER), BF16),
        compiler_params=_cparams(("parallel",)),
        name="fourier_b",
    )(zri, b_mat)
    nb = n2 // mb
    out = pl.pallas_call(
        functools.partial(_fc_kernel, n1=n1, mb=mb, out_scale=out_scale),
        grid=(nb,),
        in_specs=[
            pl.BlockSpec((mb, n1, D_FOURIER), lambda b: (b, 0, 0)),
            pl.BlockSpec((mb, n1, D_FOURIER), lambda b: (nb + b, 0, 0)),
            pl.BlockSpec((mb, n1, 2 * n1), lambda b: (b, 0, 0)),
        ],
        out_specs=pl.BlockSpec((n1, mb, D_FOURIER), lambda b: (0, b, 0)),
        out_shape=jax.ShapeDtypeStruct((n1, n2, D_FOURIER), F32),
        compiler_params=_cparams(("parallel",)),
        name="fourier_c",
    )(u, u, c_mat)
    return out.reshape(n, D_FOURIER)


def _fourier_ctx(zc):
    n = zc.shape[0]
    a_mat, _, _, out_scale = _dft_consts(n, 1)
    ii = np.arange(n)
    ang = 2 * np.pi * ((ii[:, None] * ii[None, :]) % n) / n
    c_mat = _bf16_const(np.concatenate([np.cos(ang), np.sin(ang)], axis=1))
    zri = _fourier_stage_a(zc, a_mat, tm=n)
    return pl.pallas_call(
        functools.partial(_fourier_ctx_kernel, n=n, out_scale=out_scale),
        grid=(1,),
        in_specs=[
            pl.BlockSpec((2, n, D_FOURIER), lambda i: (0, 0, 0)),
            pl.BlockSpec(c_mat.shape, lambda i: (0, 0)),
        ],
        out_specs=pl.BlockSpec((n, D_FOURIER), lambda i: (0, 0)),
        out_shape=jax.ShapeDtypeStruct((n, D_FOURIER), F32),
        compiler_params=_cparams(("arbitrary",)),
        name="fourier_ctx",
    )(zri, c_mat)


def _conv_kernel(ap_ref, ac_ref, an_ref, gp_ref, gc_ref, gn_ref, w_ref, b_ref, lg_ref, lb_ref, o_ref, win_ref,
                 rot_ref, *, tm, rs):
    i = pl.program_id(0)

    def glu(a_ref, g_ref):
        return a_ref[...].astype(F32) * jax.nn.sigmoid(g_ref[...].astype(F32))

    win_ref[0:CONV_HALO, :] = jnp.where(i == 0, 0.0, glu(ap_ref, gp_ref))
    win_ref[CONV_HALO:CONV_HALO + tm, :] = glu(ac_ref, gc_ref)
    win_ref[CONV_HALO + tm:, :] = jnp.where(i == pl.num_programs(0) - 1, 0.0, glu(an_ref, gn_ref))

    n_rot = rot_ref.shape[1]
    for s in range(8):
        rot_ref[s] = win_ref[s:s + n_rot, :]

    first = CONV_HALO - CONV_WIDTH // 2
    lanes = 128
    for r in range(tm // rs):
        parts = []
        for c in range(D_CONV // lanes):
            cs = slice(c * lanes, (c + 1) * lanes)
            acc = jnp.zeros((rs, lanes), F32)
            for t in range(CONV_WIDTH):
                off = r * rs + first + t
                base = off - off % 8
                acc = acc + rot_ref[off % 8, base:base + rs, cs] * w_ref[t:t + 1, cs]
            parts.append(acc)
        y = jnp.concatenate(parts, axis=-1) + b_ref[...]
        y = _layer_norm(y, lg_ref[...], lb_ref[...])
        o_ref[r * rs:(r + 1) * rs, :] = (y * jax.nn.sigmoid(y)).astype(BF16)


def _conv_module(zin, w_dw, b_dw, ln_g, ln_b, tm, rs=32):
    n = zin.shape[0]
    nt = n // tm
    hb = tm // CONV_HALO
    n_hb = n // CONV_HALO
    prev_map = lambda col: (lambda i: (jnp.maximum(i * hb - 1, 0), col))
    next_map = lambda col: (lambda i: (jnp.minimum((i + 1) * hb, n_hb - 1), col))
    cur_map = lambda col: (lambda i: (i, col))
    vec = lambda v: v.reshape(1, D_CONV).astype(F32)
    return pl.pallas_call(
        functools.partial(_conv_kernel, tm=tm, rs=rs),
        grid=(nt,),
        in_specs=[
            pl.BlockSpec((CONV_HALO, D_CONV), prev_map(COL_A)),
            pl.BlockSpec((tm, D_CONV), cur_map(COL_A)),
            pl.BlockSpec((CONV_HALO, D_CONV), next_map(COL_A)),
            pl.BlockSpec((CONV_HALO, D_CONV), prev_map(COL_G)),
            pl.BlockSpec((tm, D_CONV), cur_map(COL_G)),
            pl.BlockSpec((CONV_HALO, D_CONV), next_map(COL_G)),
            pl.BlockSpec((CONV_WIDTH, D_CONV), lambda i: (0, 0)),
            pl.BlockSpec((1, D_CONV), lambda i: (0, 0)),
            pl.BlockSpec((1, D_CONV), lambda i: (0, 0)),
            pl.BlockSpec((1, D_CONV), lambda i: (0, 0)),
        ],
        out_specs=pl.BlockSpec((tm, D_CONV), lambda i: (i, 0)),
        out_shape=jax.ShapeDtypeStruct((n, D_CONV), BF16),
        scratch_shapes=[pltpu.VMEM((tm + 2 * CONV_HALO, D_CONV), F32),
                        pltpu.VMEM((8, tm + 2 * CONV_HALO - 8, D_CONV), F32)],
        compiler_params=_cparams(("parallel",)),
        name="conv_module",
    )(zin, zin, zin, zin, zin, zin, w_dw.astype(F32), vec(b_dw), vec(ln_g), vec(ln_b))


def _merge_kernel(x_ref, mod_ref, yna_ref, yf_ref, yc_ref, wg0_ref, wg1_ref, wg2_ref, bg0_ref, bg1_ref, bg2_ref,
                  wna_ref, wf_ref, wc_ref, wo_ref, g_ref, b_ref, o_ref, h_ref, m_ref, acc_ref, *, nc, cw):
    k = pl.program_id(1)

    @pl.when(k == 0)
    def _():
        for c in range(nc):
            cs = slice(c * cw, (c + 1) * cw)
            h_ref[:, cs] = (x_ref[:, cs] * (1.0 + mod_ref[4:5, cs]) + mod_ref[3:4, cs]).astype(BF16)

    @pl.when(k < nc)
    def _():
        h = h_ref[...]
        g0 = jax.nn.sigmoid(_dot(h, wg0_ref[0]) + bg0_ref[...])
        m = g0 * _dot(yna_ref[...], wna_ref[0])
        g1 = jax.nn.sigmoid(_dot(h, wg1_ref[0]) + bg1_ref[...])
        m = m + g1 * _dot(yf_ref[...].astype(BF16), wf_ref[0])
        g2 = jax.nn.sigmoid(_dot(h, wg2_ref[0]) + bg2_ref[...])
        m = m + g2 * _dot(yc_ref[...], wc_ref[0])
        m_ref[k] = m.astype(BF16)

    @pl.when(k >= nc)
    def _():
        acc = _dot(m_ref[0], wo_ref[0, 0:cw, :])
        for kk in range(1, nc):
            acc = acc + _dot(m_ref[kk], wo_ref[0, kk * cw:(kk + 1) * cw, :])
        acc_ref[k - nc] = acc

    @pl.when(k == 2 * nc - 1)
    def _():
        def pre_norm(rows, cs):
            return ALPHA * x_ref[rows, cs] + mod_ref[5:6, cs] * acc_ref[cs.start // cw, rows, :]

        _residual_layer_norm(pre_norm, g_ref, b_ref, o_ref, cw, min(o_ref.shape[0], 256))


def _merge(x, mod, y_na, y_f, y_c, w_gate, b_gate, w_b_na, w_b_f, w_b_c, w_o, ln_g, ln_b, tm):
    n, d = x.shape
    nc, _, cw = w_o.shape
    ga = lambda br: (lambda i, k: (br * nc + jnp.minimum(k, nc - 1), 0, 0))
    gb = lambda br: (lambda i, k: (0, br * nc + jnp.minimum(k, nc - 1)))
    pa = lambda i, k: (jnp.minimum(k, nc - 1), 0, 0)
    pb = lambda i, k: (jnp.maximum(k - nc, 0), 0, 0)
    row = lambda i, k: (i, 0)
    fixed = lambda i, k: (0, 0)
    bg = b_gate.reshape(1, -1)
    return pl.pallas_call(
        functools.partial(_merge_kernel, nc=nc, cw=cw),
        grid=(n // tm, 2 * nc),
        in_specs=[
            pl.BlockSpec((tm, d), row),
            pl.BlockSpec(mod.shape, fixed),
            pl.BlockSpec((tm, D_NA), row),
            pl.BlockSpec((tm, D_FOURIER), row),
            pl.BlockSpec((tm, D_CONV), row),
            pl.BlockSpec((1, d, cw), ga(0)),
            pl.BlockSpec((1, d, cw), ga(1)),
            pl.BlockSpec((1, d, cw), ga(2)),
            pl.BlockSpec((1, cw), gb(0)),
            pl.BlockSpec((1, cw), gb(1)),
            pl.BlockSpec((1, cw), gb(2)),
            pl.BlockSpec((1, D_NA, cw), pa),
            pl.BlockSpec((1, D_FOURIER, cw), pa),
            pl.BlockSpec((1, D_CONV, cw), pa),
            pl.BlockSpec((1, d, cw), pb),
            pl.BlockSpec((1, d), fixed),
            pl.BlockSpec((1, d), fixed),
        ],
        out_specs=pl.BlockSpec((tm, d), row),
        out_shape=jax.ShapeDtypeStruct((n, d), F32),
        scratch_shapes=[
            pltpu.VMEM((tm, d), BF16),
            pltpu.VMEM((nc, tm, cw), BF16),
            pltpu.VMEM((nc, tm, cw), F32),
        ],
        compiler_params=_cparams(("parallel", "arbitrary")),
        name="merge",
    )(x, mod, y_na, y_f, y_c, w_gate, w_gate, w_gate, bg, bg, bg, w_b_na, w_b_f, w_b_c, w_o,
      ln_g.reshape(1, d), ln_b.reshape(1, d))


def _row_tile(n, target):
    t = min(n, target)
    assert n % t == 0
    return t


def kernel(x, c, ctx, c_ctx, w_ada, b_ada, ffn1_w1, ffn1_w3, ffn1_w2, ffn2_w1, ffn2_w3, ffn2_w2, w_in, w_dw, b_dw, conv_ln_g, conv_ln_b, rpb, w_gate, b_gate, w_b_na, w_b_f, w_b_c, w_o, ln_g, ln_b):
    b_, n, d = x.shape
    assert b_ == 1 and c.shape[0] == 1
    n_ctx = ctx.shape[1]
    depth = w_ada.shape[0]

    c_rows = jnp.zeros((8, d), F32).at[0].set(c[0]).at[1].set(c_ctx)
    mods = _ada_mods(c_rows, w_ada, b_ada)

    tm = _row_tile(n, 1024)
    tmc = _row_tile(n_ctx, 256)
    nk = D_FF // FFN_TF
    n_in = D_IN // INPROJ_TN
    nc = d // MERGE_CW

    ffn_jobs = lambda w1, w3, w2, l: [(w1, l, nk), (w3, l, nk), (w2, l, 1)]
    mix_jobs = lambda l: [(w_in, l, n_in), (w_gate, l, 3 * nc), (w_b_na, l, nc), (w_b_f, l, nc), (w_b_c, l, nc),
                          (w_o, l, nc)]
    ffn1_w = [_blocked_bf16(ffn1_w1[0], nk), _blocked_bf16(ffn1_w3[0], nk), _blocked_bf16(ffn1_w2[0], 1)]

    xl = x[0]
    xc = ctx[0]
    for l in range(depth):
        last = l == depth - 1
        mod = mods[l, 0].reshape(N_MOD_USED, d)
        modc = mods[l, 1].reshape(N_MOD_USED, d)
        conv_w = (w_dw[l], b_dw[l], conv_ln_g[l], conv_ln_b[l])

        xl, cast = _ffn(xl, mod, 0, *ffn1_w, ln_g[l, 0], ln_b[l, 0], tm=tm,
                        jobs=mix_jobs(l) + ffn_jobs(ffn2_w1, ffn2_w3, ffn2_w2, l))
        w_in_l, wg_l, wna_l, wf_l, wc_l, wo_l = cast[:6]
        ffn2_w = cast[6:]
        xc, _ = _ffn(xc, modc, 0, *ffn1_w, ln_g[l, 0], ln_b[l, 0], tm=tmc)

        zin = _inproj(xl, mod, w_in_l, tm=tm)
        zc = _inproj(xc, modc, w_in_l, tm=tmc)

        merge_w = (wg_l, b_gate[l], wna_l, wf_l, wc_l, wo_l, ln_g[l, 1], ln_b[l, 1])
        y_na = _na_attention(zin, zc, rpb[l])
        y_f = _fourier_latent(zin, n1=n // GRID_W, n2=GRID_W)
        y_c = _conv_module(zin, *conv_w, tm=_row_tile(n, 256))
        xl_new = _merge(xl, mod, y_na, y_f, y_c, *merge_w, tm=_row_tile(n, 512))

        if not last:
            yc_na = _ctx_attention(zc)
            yc_f = _fourier_ctx(zc)
            yc_c = _conv_module(zc, *conv_w, tm=n_ctx)
            xc = _merge(xc, modc, yc_na, yc_f, yc_c, *merge_w, tm=tmc)
        xl = xl_new

        jobs = [] if last else ffn_jobs(ffn1_w1, ffn1_w3, ffn1_w2, l + 1)
        xl, cast = _ffn(xl, mod, 1, *ffn2_w, ln_g[l, 2], ln_b[l, 2], tm=tm, jobs=jobs)
        if not last:
            ffn1_w = cast
            xc, _ = _ffn(xc, modc, 1, *ffn2_w, ln_g[l, 2], ln_b[l, 2], tm=tmc)
    return xl[None]
```

```python
import functools

import numpy as np
import jax
import jax.numpy as jnp
from jax import lax
from jax.experimental import pallas as pl
from jax.experimental.pallas import tpu as pltpu

F32 = jnp.float32
BF16 = jnp.bfloat16

D_MODEL = 2048
DEPTH = 2
GRID_W = 64
N_HEADS = 8
HEAD_DIM = 128
D_NA = N_HEADS * HEAD_DIM
WIN_R = 8
WIN_C = 16
FOURIER_GROUP = 128
D_FOURIER = 512
D_CONV = 512
CONV_WIDTH = 31
D_FF = 5632
D_IN = 3 * D_NA + D_FOURIER + 2 * D_CONV
N_MOD_USED = 6
ALPHA = (2 * DEPTH) ** 0.25
LN_EPS = 1e-5
NEG_INF = -1e30

COL_F = 3 * D_NA // D_FOURIER
COL_A = COL_F + 1
COL_G = COL_F + 2

LOG2E = float(np.log2(np.e))
NA_HEADS_PER_STEP = 4
Q_ROWS = 4
KW_ROWS = 12
CONV_HALO = 16
FFN_TF = 256
INPROJ_TN = 1536
MERGE_CW = 512

VMEM_LIMIT_V7X = 60 * 1024 * 1024


def _cparams(sem):
    return pltpu.CompilerParams(dimension_semantics=sem, vmem_limit_bytes=VMEM_LIMIT_V7X)


def _layer_norm(r, g, b):
    mu = jnp.mean(r, axis=-1, keepdims=True)
    d = r - mu
    var = jnp.mean(d * d, axis=-1, keepdims=True)
    return d * lax.rsqrt(var + LN_EPS) * g + b


def _dot(a, b):
    return jnp.dot(a, b, preferred_element_type=F32)


def _dot_nt(a, b):
    return lax.dot_general(a, b, (((1,), (1,)), ((), ())), preferred_element_type=F32)


def _ada_kernel(c_ref, w_ref, b_ref, o_ref):
    c = c_ref[...]
    s = (c * jax.nn.sigmoid(c)).astype(BF16)
    o_ref[0] = _dot(s, w_ref[0].astype(BF16)) + b_ref[0]


def _ada_mods(c_rows, w_ada, b_ada, tn=1024):
    n_l, d, _ = w_ada.shape
    n_out = N_MOD_USED * d
    return pl.pallas_call(
        _ada_kernel,
        grid=(n_l, n_out // tn),
        in_specs=[
            pl.BlockSpec((8, d), lambda l, j: (0, 0)),
            pl.BlockSpec((1, d, tn), lambda l, j: (l, 0, j)),
            pl.BlockSpec((1, 1, tn), lambda l, j: (l, 0, j)),
        ],
        out_specs=pl.BlockSpec((1, 8, tn), lambda l, j: (l, 0, j)),
        out_shape=jax.ShapeDtypeStruct((n_l, 8, n_out), F32),
        compiler_params=_cparams(("parallel", "arbitrary")),
        name="ada_mods",
    )(c_rows, w_ada, b_ada.reshape(n_l, 1, -1))


def _residual_layer_norm(pre_norm, g_ref, b_ref, o_ref, cw, rsub):
    tm, d = o_ref.shape
    chunks = [slice(c * cw, (c + 1) * cw) for c in range(d // cw)]

    def body(r, carry):
        rows = pl.ds(pl.multiple_of(r * rsub, rsub), rsub)
        total = 0.0
        for cs in chunks:
            v = pre_norm(rows, cs)
            o_ref[rows, cs] = v
            total = total + jnp.sum(v, axis=-1, keepdims=True)
        mu = total / d
        sq = 0.0
        for cs in chunks:
            dev = o_ref[rows, cs] - mu
            sq = sq + jnp.sum(dev * dev, axis=-1, keepdims=True)
        inv = lax.rsqrt(sq / d + LN_EPS)
        for cs in chunks:
            o_ref[rows, cs] = (o_ref[rows, cs] - mu) * inv * g_ref[:, cs] + b_ref[:, cs]
        return carry

    lax.fori_loop(0, tm // rsub, body, 0)


def _cast_jobs(src_refs, dst_refs):
    for s_ref, d_ref in zip(src_refs, dst_refs):
        w = d_ref.shape[2]
        for c in range(d_ref.shape[0]):
            d_ref[c] = s_ref[0, :, c * w:(c + 1) * w].astype(BF16)


def _cast_job_specs(src, layer, cb, n_steps, step_of):
    _, r, c = src.shape
    rb = next(v for v in range(16, r + 1, 16) if r % v == 0 and r // v <= n_steps)
    nb = r // rb
    blk = lambda *g: jnp.minimum(step_of(*g), nb - 1)
    in_spec = pl.BlockSpec((1, rb, c), lambda *g: (layer, blk(*g), 0))
    out_spec = pl.BlockSpec((cb, rb, c // cb), lambda *g: (0, blk(*g), 0))
    return in_spec, out_spec, jax.ShapeDtypeStruct((cb, r, c // cb), BF16)


def _blocked_bf16(w, cb):
    r, c = w.shape
    return w.astype(BF16).reshape(r, cb, c // cb).transpose(1, 0, 2)


def _ffn_kernel(*refs, j, cw, rsub, n_jobs):
    z_ref, mod_ref, w1_ref, w3_ref, w2_ref, g_ref, b_ref = refs[:7]
    src_refs = refs[7:7 + n_jobs]
    o_ref = refs[7 + n_jobs]
    dst_refs = refs[8 + n_jobs:8 + 2 * n_jobs]
    h_ref = refs[8 + 2 * n_jobs]
    k = pl.program_id(1)
    d = o_ref.shape[1]
    chunks = [slice(c * cw, (c + 1) * cw) for c in range(d // cw)]

    @pl.when(k == 0)
    def _():
        for cs in chunks:
            shift = mod_ref[3 * j:3 * j + 1, cs]
            scale = mod_ref[3 * j + 1:3 * j + 2, cs]
            h_ref[:, cs] = (z_ref[:, cs] * (1.0 + scale) + shift).astype(BF16)
            o_ref[:, cs] = jnp.zeros((o_ref.shape[0], cw), F32)

    _cast_jobs(src_refs, dst_refs)
    h = h_ref[...]
    a = _dot(h, w1_ref[0])
    b = _dot(h, w3_ref[0])
    u = (a * jax.nn.sigmoid(a) * b).astype(BF16)

    for cs in chunks:
        o_ref[:, cs] += _dot(u, w2_ref[0, :, cs])

    @pl.when(k == pl.num_programs(1) - 1)
    def _():
        def pre_norm(rows, cs):
            gate = mod_ref[3 * j + 2:3 * j + 3, cs]
            return ALPHA * z_ref[rows, cs] + (0.5 * gate) * o_ref[rows, cs]

        _residual_layer_norm(pre_norm, g_ref, b_ref, o_ref, cw, rsub)


def _ffn(z, mod, j, w1, w3, w2, ln_g, ln_b, tm, jobs=(), cw=512):
    n, d = z.shape
    nk, _, tf = w1.shape
    step_of = lambda i, k: i * nk + k
    job_specs = [_cast_job_specs(src, layer, cb, (n // tm) * nk, step_of) for src, layer, cb in jobs]
    outs = pl.pallas_call(
        functools.partial(_ffn_kernel, j=j, cw=cw, rsub=min(tm, 256), n_jobs=len(jobs)),
        grid=(n // tm, nk),
        in_specs=[
            pl.BlockSpec((tm, d), lambda i, k: (i, 0)),
            pl.BlockSpec(mod.shape, lambda i, k: (0, 0)),
            pl.BlockSpec((1, d, tf), lambda i, k: (k, 0, 0)),
            pl.BlockSpec((1, d, tf), lambda i, k: (k, 0, 0)),
            pl.BlockSpec((1, tf, d), lambda i, k: (0, k, 0)),
            pl.BlockSpec((1, d), lambda i, k: (0, 0)),
            pl.BlockSpec((1, d), lambda i, k: (0, 0)),
        ] + [js[0] for js in job_specs],
        out_specs=[pl.BlockSpec((tm, d), lambda i, k: (i, 0))] + [js[1] for js in job_specs],
        out_shape=[jax.ShapeDtypeStruct((n, d), F32)] + [js[2] for js in job_specs],
        scratch_shapes=[pltpu.VMEM((tm, d), BF16)],
        compiler_params=_cparams(("arbitrary", "arbitrary")),
        name="ffn",
    )(z, mod, w1, w3, w2, ln_g.reshape(1, d), ln_b.reshape(1, d), *[src for src, _, _ in jobs])
    return outs[0], list(outs[1:])


def _inproj_kernel(x_ref, mod_ref, w_ref, o_ref, h_ref, *, cw):
    @pl.when(pl.program_id(1) == 0)
    def _():
        for c in range(x_ref.shape[1] // cw):
            cs = slice(c * cw, (c + 1) * cw)
            h_ref[:, cs] = (x_ref[:, cs] * (1.0 + mod_ref[4:5, cs]) + mod_ref[3:4, cs]).astype(BF16)

    o_ref[...] = _dot(h_ref[...], w_ref[0]).astype(BF16)


def _inproj(x, mod, w_in, tm, cw=512):
    n, d = x.shape
    nb, _, tn = w_in.shape
    d_in = nb * tn
    return pl.pallas_call(
        functools.partial(_inproj_kernel, cw=cw),
        grid=(n // tm, nb),
        in_specs=[
            pl.BlockSpec((tm, d), lambda i, c: (i, 0)),
            pl.BlockSpec(mod.shape, lambda i, c: (0, 0)),
            pl.BlockSpec((1, d, tn), lambda i, c: (c, 0, 0)),
        ],
        out_specs=[pl.BlockSpec((tm, tn), lambda i, c: (i, c)), pl.BlockSpec((tm, d), lambda i, c: (i, 0))],
        out_shape=[jax.ShapeDtypeStruct((n, d_in), BF16), jax.ShapeDtypeStruct((n, d), BF16)],
        compiler_params=_cparams(("parallel", "arbitrary")),
        name="inproj",
    )(x, mod, w_in)


def _na_geometry(rows):
    assert rows % Q_ROWS == 0 and rows >= KW_ROWS and KW_ROWS % 2 == 0
    wr = min(WIN_R, rows)
    n_rb = rows // Q_ROWS
    q0 = np.arange(n_rb) * Q_ROWS
    kstart = np.clip(q0 - WIN_R // 2, 0, rows - KW_ROWS)
    offs = kstart - q0
    uniq = sorted(set(offs.tolist()), reverse=True)
    cfg = np.array([uniq.index(o) for o in offs.tolist()], np.int32)
    dr_cfg, ok_cfg = [], []
    for off in uniq:
        rb = int(np.nonzero(offs == off)[0][0])
        rq = q0[rb] + np.arange(Q_ROWS)
        rk = kstart[rb] + np.arange(KW_ROWS)
        r0 = np.clip(rq - wr // 2, 0, rows - wr)
        ok_cfg.append((rk[None, :] >= r0[:, None]) & (rk[None, :] < r0[:, None] + wr))
        dr_cfg.append(rk[None, :] - rq[:, None] + WIN_R - 1)
    return n_rb, kstart.astype(np.int32), cfg, np.stack(dr_cfg), np.stack(ok_cfg)


def _na_build_bias(h, rpb_ref, t_ref, bias_ref, dr_cfg, ok_cfg):
    n_dr, n_dc = 2 * WIN_R - 1, 2 * WIN_C - 1
    cq = lax.broadcasted_iota(jnp.int32, (GRID_W, 2 * GRID_W), 0)
    lane = lax.broadcasted_iota(jnp.int32, (GRID_W, 2 * GRID_W), 1)
    ck = lane & (GRID_W - 1)
    c0 = jnp.clip(cq - WIN_C // 2, 0, GRID_W - WIN_C)
    col_ok = (ck >= c0) & (ck < c0 + WIN_C)
    dc = jnp.clip(ck - cq, -(WIN_C - 1), WIN_C - 1) + WIN_C - 1
    base = h * (n_dr * n_dc)
    for dr in range(n_dr):
        acc = jnp.zeros((GRID_W, 2 * GRID_W), F32)
        for j in range(n_dc):
            acc = jnp.where(dc == j, rpb_ref[base + dr * n_dc + j], acc)
        t_ref[dr] = jnp.where(col_ok, acc * LOG2E, NEG_INF)
    left = lane < GRID_W
    neg = jnp.full((GRID_W, 2 * GRID_W), NEG_INF, F32)
    for c in range(dr_cfg.shape[0]):
        for rq in range(Q_ROWS):
            for rk in range(0, KW_ROWS, 2):
                even = t_ref[int(dr_cfg[c, rq, rk])] if ok_cfg[c, rq, rk] else neg
                odd = t_ref[int(dr_cfg[c, rq, rk + 1])] if ok_cfg[c, rq, rk + 1] else neg
                bias_ref[c, rq * GRID_W:(rq + 1) * GRID_W, rk * GRID_W:(rk + 2) * GRID_W] = jnp.where(left, even, odd)


def _na_kernel(cfg_ref, ks_ref, rpb_ref, q_ref, k_ref, v_ref, kc_ref, vc_ref, o_ref, t_ref, bias_ref,
               *, dr_cfg, ok_cfg):
    hg = pl.program_id(0)
    rb = pl.program_id(1)

    @pl.when(rb == 0)
    def _():
        for hh in range(NA_HEADS_PER_STEP):
            _na_build_bias(hg * NA_HEADS_PER_STEP + hh, rpb_ref, t_ref, bias_ref.at[hh], dr_cfg, ok_cfg)

    k0 = pl.multiple_of(ks_ref[rb] * GRID_W, GRID_W)
    cfg = cfg_ref[rb]
    scale = (HEAD_DIM ** -0.5) * LOG2E
    for hh in range(NA_HEADS_PER_STEP):
        hs = slice(hh * HEAD_DIM, (hh + 1) * HEAD_DIM)
        q = q_ref[:, hs]
        kwin = k_ref[pl.ds(k0, KW_ROWS * GRID_W), hs]
        vwin = v_ref[pl.ds(k0, KW_ROWS * GRID_W), hs]
        s_lat = _dot_nt(q, kwin) * scale + bias_ref[hh, cfg]
        s_ctx = _dot_nt(q, kc_ref[:, hs]) * scale
        m = jnp.maximum(jnp.max(s_lat, axis=-1, keepdims=True), jnp.max(s_ctx, axis=-1, keepdims=True))
        p_lat = jnp.exp2(s_lat - m)
        p_ctx = jnp.exp2(s_ctx - m)
        denom = jnp.sum(p_lat, axis=-1, keepdims=True) + jnp.sum(p_ctx, axis=-1, keepdims=True)
        o = _dot(p_lat.astype(BF16), vwin) + _dot(p_ctx.astype(BF16), vc_ref[:, hs])
        o_ref[:, hs] = (o / denom).astype(BF16)


def _na_attention(zin, zc, rpb_l):
    n = zin.shape[0]
    n_ctx = zc.shape[0]
    rows = n // GRID_W
    n_rb, kstart, cfg, dr_cfg, ok_cfg = _na_geometry(rows)
    tq = Q_ROWS * GRID_W
    hps = NA_HEADS_PER_STEP
    wb = hps * HEAD_DIM
    n_hg = N_HEADS // hps
    grid_spec = pltpu.PrefetchScalarGridSpec(
        num_scalar_prefetch=3,
        grid=(n_hg, n_rb),
        in_specs=[
            pl.BlockSpec((tq, wb), lambda g, r, *_: (r, g)),
            pl.BlockSpec((n, wb), lambda g, r, *_: (0, n_hg + g)),
            pl.BlockSpec((n, wb), lambda g, r, *_: (0, 2 * n_hg + g)),
            pl.BlockSpec((n_ctx, wb), lambda g, r, *_: (0, n_hg + g)),
            pl.BlockSpec((n_ctx, wb), lambda g, r, *_: (0, 2 * n_hg + g)),
        ],
        out_specs=pl.BlockSpec((tq, wb), lambda g, r, *_: (r, g)),
        scratch_shapes=[
            pltpu.VMEM((2 * WIN_R - 1, GRID_W, 2 * GRID_W), F32),
            pltpu.VMEM((hps, dr_cfg.shape[0], tq, KW_ROWS * GRID_W), F32),
        ],
    )
    return pl.pallas_call(
        functools.partial(_na_kernel, dr_cfg=dr_cfg, ok_cfg=ok_cfg),
        grid_spec=grid_spec,
        out_shape=jax.ShapeDtypeStruct((n, D_NA), BF16),
        compiler_params=_cparams(("arbitrary", "arbitrary")),
        name="na_attention",
    )(jnp.asarray(cfg), jnp.asarray(kstart), rpb_l.astype(F32).reshape(-1), zin, zin, zin, zc, zc)


def _ctx_attn_kernel(q_ref, k_ref, v_ref, o_ref):
    s = _dot_nt(q_ref[...], k_ref[...]) * (HEAD_DIM ** -0.5)
    p = jnp.exp(s - jnp.max(s, axis=-1, keepdims=True))
    o = _dot(p.astype(BF16), v_ref[...])
    o_ref[...] = (o / jnp.sum(p, axis=-1, keepdims=True)).astype(BF16)


def _ctx_attention(zc):
    n_ctx = zc.shape[0]
    hb = D_NA // HEAD_DIM
    return pl.pallas_call(
        _ctx_attn_kernel,
        grid=(N_HEADS,),
        in_specs=[
            pl.BlockSpec((n_ctx, HEAD_DIM), lambda h: (0, h)),
            pl.BlockSpec((n_ctx, HEAD_DIM), lambda h: (0, hb + h)),
            pl.BlockSpec((n_ctx, HEAD_DIM), lambda h: (0, 2 * hb + h)),
        ],
        out_specs=pl.BlockSpec((n_ctx, HEAD_DIM), lambda h: (0, h)),
        out_shape=jax.ShapeDtypeStruct((n_ctx, D_NA), BF16),
        compiler_params=_cparams(("parallel",)),
        name="ctx_attention",
    )(zc, zc, zc)


def _pow2_near(v):
    return 2.0 ** np.round(np.log2(v))


def _bf16_const(a):
    return jnp.asarray(a, F32).astype(BF16)


def _dft_consts(n1, n2):
    k = FOURIER_GROUP
    n_g = D_FOURIER // k
    n = n1 * n2
    sa, sb = 1.0 / _pow2_near(np.sqrt(k)), 1.0 / _pow2_near(np.sqrt(n2))
    jj = np.arange(k)
    ang = 2 * np.pi * ((jj[:, None] * jj[None, :]) % k) / k
    eye = np.eye(n_g)
    a_mat = np.concatenate([np.kron(eye, np.cos(ang)), -np.kron(eye, np.sin(ang))], axis=1) * sa
    ii = np.arange(n2)
    ang2 = 2 * np.pi * ((ii[:, None] * ii[None, :]) % n2) / n2
    c2, s2 = np.cos(ang2), np.sin(ang2)
    b_mat = np.block([[c2, s2], [-s2, c2]]) * sb
    m2 = np.arange(n2)[:, None, None]
    m1 = np.arange(n1)[None, :, None]
    nn1 = np.arange(n1)[None, None, :]
    ang3 = 2 * np.pi * ((nn1 * (n2 * m1 + m2)) % n) / n
    c_mat = np.concatenate([np.cos(ang3), np.sin(ang3)], axis=2)
    out_scale = float(1.0 / (np.sqrt(k * n) * sa * sb))
    return _bf16_const(a_mat), _bf16_const(b_mat), _bf16_const(c_mat), out_scale


def _fa_kernel(f_ref, a_ref, o_ref):
    z = _dot(f_ref[...], a_ref[...])
    o_ref[0] = z[:, :D_FOURIER].astype(BF16)
    o_ref[1] = z[:, D_FOURIER:].astype(BF16)


def _fb_kernel(z_ref, b_ref, o_ref):
    for j in range(z_ref.shape[1]):
        o_ref[:, j, :] = _dot(b_ref[...], z_ref[:, j, :]).astype(BF16)


def _fc_kernel(ur_ref, ui_ref, c_ref, o_ref, *, n1, mb, out_scale):
    for j in range(mb):
        cm = c_ref[j]
        res = _dot(cm[:, :n1], ur_ref[j]) + _dot(cm[:, n1:], ui_ref[j])
        o_ref[:, j, :] = res * out_scale


def _fourier_ctx_kernel(z_ref, c_ref, o_ref, *, n, out_scale):
    o_ref[...] = (_dot(c_ref[:, :n], z_ref[0]) + _dot(c_ref[:, n:], z_ref[1])) * out_scale


def _fourier_stage_a(zin, a_mat, tm):
    n = zin.shape[0]
    return pl.pallas_call(
        _fa_kernel,
        grid=(n // tm,),
        in_specs=[
            pl.BlockSpec((tm, D_FOURIER), lambda i: (i, COL_F)),
            pl.BlockSpec(a_mat.shape, lambda i: (0, 0)),
        ],
        out_specs=pl.BlockSpec((2, tm, D_FOURIER), lambda i: (0, i, 0)),
        out_shape=jax.ShapeDtypeStruct((2, n, D_FOURIER), BF16),
        compiler_params=_cparams(("parallel",)),
        name="fourier_a",
    )(zin, a_mat)


def _fourier_latent(zin, n1, n2, jt=16, mb=8):
    n = zin.shape[0]
    assert n == n1 * n2
    a_mat, b_mat, c_mat, out_scale = _dft_consts(n1, n2)
    zri = _fourier_stage_a(zin, a_mat, tm=min(n, 1024)).reshape(2 * n2, n1, D_FOURIER)
    jt = min(jt, n1)
    u = pl.pallas_call(
        _fb_kernel,
        grid=(n1 // jt,),
        in_specs=[
            pl.BlockSpec((2 * n2, jt, D_FOURIER), lambda c: (0, c, 0)),
            pl.BlockSpec(b_mat.shape, lambda c: (0, 0)),
        ],
        out_specs=pl.BlockSpec((2 * n2, jt, D_FOURIER), lambda c: (0, c, 0)),
        out_shape=jax.ShapeDtypeStruct((2 * n2, n1, D_FOURIER), BF16),
        compiler_params=_cparams(("parallel",)),
        name="fourier_b",
    )(zri, b_mat)
    nb = n2 // mb
    out = pl.pallas_call(
        functools.partial(_fc_kernel, n1=n1, mb=mb, out_scale=out_scale),
        grid=(nb,),
        in_specs=[
            pl.BlockSpec((mb, n1, D_FOURIER), lambda b: (b, 0, 0)),
            pl.BlockSpec((mb, n1, D_FOURIER), lambda b: (nb + b, 0, 0)),
            pl.BlockSpec((mb, n1, 2 * n1), lambda b: (b, 0, 0)),
        ],
        out_specs=pl.BlockSpec((n1, mb, D_FOURIER), lambda b: (0, b, 0)),
        out_shape=jax.ShapeDtypeStruct((n1, n2, D_FOURIER), F32),
        compiler_params=_cparams(("parallel",)),
        name="fourier_c",
    )(u, u, c_mat)
    return out.reshape(n, D_FOURIER)


def _fourier_ctx(zc):
    n = zc.shape[0]
    a_mat, _, _, out_scale = _dft_consts(n, 1)
    ii = np.arange(n)
    ang = 2 * np.pi * ((ii[:, None] * ii[None, :]) % n) / n
    c_mat = _bf16_const(np.concatenate([np.cos(ang), np.sin(ang)], axis=1))
    zri = _fourier_stage_a(zc, a_mat, tm=n)
    return pl.pallas_call(
        functools.partial(_fourier_ctx_kernel, n=n, out_scale=out_scale),
        grid=(1,),
        in_specs=[
            pl.BlockSpec((2, n, D_FOURIER), lambda i: (0, 0, 0)),
            pl.BlockSpec(c_mat.shape, lambda i: (0, 0)),
        ],
        out_specs=pl.BlockSpec((n, D_FOURIER), lambda i: (0, 0)),
        out_shape=jax.ShapeDtypeStruct((n, D_FOURIER), F32),
        compiler_params=_cparams(("arbitrary",)),
        name="fourier_ctx",
    )(zri, c_mat)


def _conv_kernel(ap_ref, ac_ref, an_ref, gp_ref, gc_ref, gn_ref, w_ref, b_ref, lg_ref, lb_ref, o_ref, win_ref,
                 rot_ref, *, tm, rs):
    i = pl.program_id(0)

    def glu(a_ref, g_ref):
        return a_ref[...].astype(F32) * jax.nn.sigmoid(g_ref[...].astype(F32))

    win_ref[0:CONV_HALO, :] = jnp.where(i == 0, 0.0, glu(ap_ref, gp_ref))
    win_ref[CONV_HALO:CONV_HALO + tm, :] = glu(ac_ref, gc_ref)
    win_ref[CONV_HALO + tm:, :] = jnp.where(i == pl.num_programs(0) - 1, 0.0, glu(an_ref, gn_ref))

    n_rot = rot_ref.shape[1]
    for s in range(8):
        rot_ref[s] = win_ref[s:s + n_rot, :]

    first = CONV_HALO - CONV_WIDTH // 2
    lanes = 128
    for r in range(tm // rs):
        parts = []
        for c in range(D_CONV // lanes):
            cs = slice(c * lanes, (c + 1) * lanes)
            acc = jnp.zeros((rs, lanes), F32)
            for t in range(CONV_WIDTH):
                off = r * rs + first + t
                base = off - off % 8
                acc = acc + rot_ref[off % 8, base:base + rs, cs] * w_ref[t:t + 1, cs]
            parts.append(acc)
        y = jnp.concatenate(parts, axis=-1) + b_ref[...]
        y = _layer_norm(y, lg_ref[...], lb_ref[...])
        o_ref[r * rs:(r + 1) * rs, :] = (y * jax.nn.sigmoid(y)).astype(BF16)


def _conv_module(zin, w_dw, b_dw, ln_g, ln_b, tm, rs=32):
    n = zin.shape[0]
    nt = n // tm
    hb = tm // CONV_HALO
    n_hb = n // CONV_HALO
    prev_map = lambda col: (lambda i: (jnp.maximum(i * hb - 1, 0), col))
    next_map = lambda col: (lambda i: (jnp.minimum((i + 1) * hb, n_hb - 1), col))
    cur_map = lambda col: (lambda i: (i, col))
    vec = lambda v: v.reshape(1, D_CONV).astype(F32)
    return pl.pallas_call(
        functools.partial(_conv_kernel, tm=tm, rs=rs),
        grid=(nt,),
        in_specs=[
            pl.BlockSpec((CONV_HALO, D_CONV), prev_map(COL_A)),
            pl.BlockSpec((tm, D_CONV), cur_map(COL_A)),
            pl.BlockSpec((CONV_HALO, D_CONV), next_map(COL_A)),
            pl.BlockSpec((CONV_HALO, D_CONV), prev_map(COL_G)),
            pl.BlockSpec((tm, D_CONV), cur_map(COL_G)),
            pl.BlockSpec((CONV_HALO, D_CONV), next_map(COL_G)),
            pl.BlockSpec((CONV_WIDTH, D_CONV), lambda i: (0, 0)),
            pl.BlockSpec((1, D_CONV), lambda i: (0, 0)),
            pl.BlockSpec((1, D_CONV), lambda i: (0, 0)),
            pl.BlockSpec((1, D_CONV), lambda i: (0, 0)),
        ],
        out_specs=pl.BlockSpec((tm, D_CONV), lambda i: (i, 0)),
        out_shape=jax.ShapeDtypeStruct((n, D_CONV), BF16),
        scratch_shapes=[pltpu.VMEM((tm + 2 * CONV_HALO, D_CONV), F32),
                        pltpu.VMEM((8, tm + 2 * CONV_HALO - 8, D_CONV), F32)],
        compiler_params=_cparams(("parallel",)),
        name="conv_module",
    )(zin, zin, zin, zin, zin, zin, w_dw.astype(F32), vec(b_dw), vec(ln_g), vec(ln_b))


def _gate_mix_kernel(h_ref, yna_ref, yf_ref, yc_ref, wg0_ref, wg1_ref, wg2_ref, bg0_ref, bg1_ref, bg2_ref,
                     wna_ref, wf_ref, wc_ref, m_ref):
    h = h_ref[...]
    g0 = jax.nn.sigmoid(_dot(h, wg0_ref[0]) + bg0_ref[...])
    m = g0 * _dot(yna_ref[...], wna_ref[0])
    g1 = jax.nn.sigmoid(_dot(h, wg1_ref[0]) + bg1_ref[...])
    m = m + g1 * _dot(yf_ref[...].astype(BF16), wf_ref[0])
    g2 = jax.nn.sigmoid(_dot(h, wg2_ref[0]) + bg2_ref[...])
    m = m + g2 * _dot(yc_ref[...], wc_ref[0])
    m_ref[...] = m.astype(BF16)


def _gate_mix(h, y_na, y_f, y_c, w_gate, b_gate, w_b_na, w_b_f, w_b_c, tm):
    n, d = h.shape
    nc, _, cw = w_b_na.shape
    ga = lambda br: (lambda c, i: (br * nc + c, 0, 0))
    gb = lambda br: (lambda c, i: (0, br * nc + c))
    pa = lambda c, i: (c, 0, 0)
    row = lambda c, i: (i, 0)
    bg = b_gate.reshape(1, -1)
    return pl.pallas_call(
        _gate_mix_kernel,
        grid=(nc, n // tm),
        in_specs=[
            pl.BlockSpec((tm, d), row),
            pl.BlockSpec((tm, D_NA), row),
            pl.BlockSpec((tm, D_FOURIER), row),
            pl.BlockSpec((tm, D_CONV), row),
            pl.BlockSpec((1, d, cw), ga(0)),
            pl.BlockSpec((1, d, cw), ga(1)),
            pl.BlockSpec((1, d, cw), ga(2)),
            pl.BlockSpec((1, cw), gb(0)),
            pl.BlockSpec((1, cw), gb(1)),
            pl.BlockSpec((1, cw), gb(2)),
            pl.BlockSpec((1, D_NA, cw), pa),
            pl.BlockSpec((1, D_FOURIER, cw), pa),
            pl.BlockSpec((1, D_CONV, cw), pa),
        ],
        out_specs=pl.BlockSpec((tm, cw), lambda c, i: (i, c)),
        out_shape=jax.ShapeDtypeStruct((n, d), BF16),
        compiler_params=_cparams(("parallel", "arbitrary")),
        name="gate_mix",
    )(h, y_na, y_f, y_c, w_gate, w_gate, w_gate, bg, bg, bg, w_b_na, w_b_f, w_b_c)


def _out_proj_kernel(m_ref, x_ref, mod_ref, wo_ref, g_ref, b_ref, o_ref):
    nc, _, cw = wo_ref.shape
    m = m_ref[...]
    for c in range(nc):
        cs = slice(c * cw, (c + 1) * cw)
        o_ref[:, cs] = ALPHA * x_ref[:, cs] + mod_ref[5:6, cs] * _dot(m, wo_ref[c])
    _residual_layer_norm(lambda rows, cs: o_ref[rows, cs], g_ref, b_ref, o_ref, cw, min(o_ref.shape[0], 256))


def _out_proj(m, x, mod, w_o, ln_g, ln_b, tm):
    n, d = x.shape
    row = lambda i: (i, 0)
    fixed = lambda i: (0, 0)
    return pl.pallas_call(
        _out_proj_kernel,
        grid=(n // tm,),
        in_specs=[
            pl.BlockSpec((tm, d), row),
            pl.BlockSpec((tm, d), row),
            pl.BlockSpec(mod.shape, fixed),
            pl.BlockSpec(w_o.shape, lambda i: (0, 0, 0)),
            pl.BlockSpec((1, d), fixed),
            pl.BlockSpec((1, d), fixed),
        ],
        out_specs=pl.BlockSpec((tm, d), row),
        out_shape=jax.ShapeDtypeStruct((n, d), F32),
        compiler_params=_cparams(("parallel",)),
        name="out_proj",
    )(m, x, mod, w_o, ln_g.reshape(1, d), ln_b.reshape(1, d))


def _merge(x, h, mod, y_na, y_f, y_c, w_gate, b_gate, w_b_na, w_b_f, w_b_c, w_o, ln_g, ln_b, tm_mix, tm_out):
    m = _gate_mix(h, y_na, y_f, y_c, w_gate, b_gate, w_b_na, w_b_f, w_b_c, tm=tm_mix)
    return _out_proj(m, x, mod, w_o, ln_g, ln_b, tm=tm_out)


def _row_tile(n, target):
    t = min(n, target)
    assert n % t == 0
    return t


def kernel(x, c, ctx, c_ctx, w_ada, b_ada, ffn1_w1, ffn1_w3, ffn1_w2, ffn2_w1, ffn2_w3, ffn2_w2, w_in, w_dw, b_dw, conv_ln_g, conv_ln_b, rpb, w_gate, b_gate, w_b_na, w_b_f, w_b_c, w_o, ln_g, ln_b):
    b_, n, d = x.shape
    assert b_ == 1 and c.shape[0] == 1
    n_ctx = ctx.shape[1]
    depth = w_ada.shape[0]

    c_rows = jnp.zeros((8, d), F32).at[0].set(c[0]).at[1].set(c_ctx)
    mods = _ada_mods(c_rows, w_ada, b_ada)

    tm = _row_tile(n, 1024)
    tmc = _row_tile(n_ctx, 256)
    nk = D_FF // FFN_TF
    n_in = D_IN // INPROJ_TN
    nc = d // MERGE_CW

    ffn_jobs = lambda w1, w3, w2, l: [(w1, l, nk), (w3, l, nk), (w2, l, 1)]
    mix_jobs = lambda l: [(w_in, l, n_in), (w_gate, l, 3 * nc), (w_b_na, l, nc), (w_b_f, l, nc), (w_b_c, l, nc),
                          (w_o, l, nc)]
    ffn1_w = [_blocked_bf16(ffn1_w1[0], nk), _blocked_bf16(ffn1_w3[0], nk), _blocked_bf16(ffn1_w2[0], 1)]

    xl = x[0]
    xc = ctx[0]
    for l in range(depth):
        last = l == depth - 1
        mod = mods[l, 0].reshape(N_MOD_USED, d)
        modc = mods[l, 1].reshape(N_MOD_USED, d)
        conv_w = (w_dw[l], b_dw[l], conv_ln_g[l], conv_ln_b[l])

        xl, cast = _ffn(xl, mod, 0, *ffn1_w, ln_g[l, 0], ln_b[l, 0], tm=tm,
                        jobs=mix_jobs(l) + ffn_jobs(ffn2_w1, ffn2_w3, ffn2_w2, l))
        w_in_l, wg_l, wna_l, wf_l, wc_l, wo_l = cast[:6]
        ffn2_w = cast[6:]
        xc, _ = _ffn(xc, modc, 0, *ffn1_w, ln_g[l, 0], ln_b[l, 0], tm=tmc)

        zin, hl = _inproj(xl, mod, w_in_l, tm=tm)
        zc, hc = _inproj(xc, modc, w_in_l, tm=tmc)

        merge_w = (wg_l, b_gate[l], wna_l, wf_l, wc_l, wo_l, ln_g[l, 1], ln_b[l, 1])
        y_na = _na_attention(zin, zc, rpb[l])
        y_f = _fourier_latent(zin, n1=n // GRID_W, n2=GRID_W)
        y_c = _conv_module(zin, *conv_w, tm=_row_tile(n, 256))
        xl_new = _merge(xl, hl, mod, y_na, y_f, y_c, *merge_w, tm_mix=tm, tm_out=_row_tile(n, 512))

        if not last:
            yc_na = _ctx_attention(zc)
            yc_f = _fourier_ctx(zc)
            yc_c = _conv_module(zc, *conv_w, tm=n_ctx)
            xc = _merge(xc, hc, modc, yc_na, yc_f, yc_c, *merge_w, tm_mix=tmc, tm_out=tmc)
        xl = xl_new

        jobs = [] if last else ffn_jobs(ffn1_w1, ffn1_w3, ffn1_w2, l + 1)
        xl, cast = _ffn(xl, mod, 1, *ffn2_w, ln_g[l, 2], ln_b[l, 2], tm=tm, jobs=jobs)
        if not last:
            ffn1_w = cast
            xc, _ = _ffn(xc, modc, 1, *ffn2_w, ln_g[l, 2], ln_b[l, 2], tm=tmc)
    return xl[None]
```

```python
import functools

import numpy as np
import jax
import jax.numpy as jnp
from jax import lax
from jax.experimental import pallas as pl
from jax.experimental.pallas import tpu as pltpu

F32 = jnp.float32
BF16 = jnp.bfloat16

D_MODEL = 2048
DEPTH = 2
GRID_W = 64
N_HEADS = 8
HEAD_DIM = 128
D_NA = N_HEADS * HEAD_DIM
WIN_R = 8
WIN_C = 16
FOURIER_GROUP = 128
D_FOURIER = 512
D_CONV = 512
CONV_WIDTH = 31
D_FF = 5632
D_IN = 3 * D_NA + D_FOURIER + 2 * D_CONV
N_MOD_USED = 6
ALPHA = (2 * DEPTH) ** 0.25
LN_EPS = 1e-5
NEG_INF = -1e30

COL_F = 3 * D_NA // D_FOURIER
COL_A = COL_F + 1
COL_G = COL_F + 2

LOG2E = float(np.log2(np.e))
NA_HEADS_PER_STEP = 4
Q_ROWS = 4
KW_ROWS = 12
CONV_HALO = 16
FFN_TF = 256
CTX_FFN_KB = 2
ADA_RK = 128
INPROJ_TN = 1536
MERGE_CW = 512

VMEM_LIMIT_V7X = 60 * 1024 * 1024


def _cparams(sem):
    return pltpu.CompilerParams(dimension_semantics=sem, vmem_limit_bytes=VMEM_LIMIT_V7X)


def _layer_norm(r, g, b):
    mu = jnp.mean(r, axis=-1, keepdims=True)
    d = r - mu
    var = jnp.mean(d * d, axis=-1, keepdims=True)
    return d * lax.rsqrt(var + LN_EPS) * g + b


def _dot(a, b):
    return jnp.dot(a, b, preferred_element_type=F32)


def _dot_nt(a, b):
    return lax.dot_general(a, b, (((1,), (1,)), ((), ())), preferred_element_type=F32)


def _ada_kernel(c_ref, w_ref, b_ref, o_ref):
    @pl.when(pl.program_id(1) == 0)
    def _():
        o_ref[0] = jnp.broadcast_to(b_ref[0], o_ref.shape[1:])

    c = c_ref[...]
    s = (c * jax.nn.sigmoid(c)).astype(BF16)
    o_ref[0] += _dot(s, w_ref[0].astype(BF16))


def _ada_mods(c_rows, w_ada, b_ada, rk=ADA_RK):
    n_l, d, _ = w_ada.shape
    n_out = N_MOD_USED * d
    return pl.pallas_call(
        _ada_kernel,
        grid=(n_l, d // rk),
        in_specs=[
            pl.BlockSpec((8, rk), lambda l, k: (0, k)),
            pl.BlockSpec((1, rk, n_out), lambda l, k: (l, k, 0)),
            pl.BlockSpec((1, 1, n_out), lambda l, k: (l, 0, 0)),
        ],
        out_specs=pl.BlockSpec((1, 8, n_out), lambda l, k: (l, 0, 0)),
        out_shape=jax.ShapeDtypeStruct((n_l, 8, n_out), F32),
        compiler_params=_cparams(("parallel", "arbitrary")),
        name="ada_mods",
    )(c_rows, w_ada, b_ada.reshape(n_l, 1, -1))


def _residual_layer_norm(pre_norm, g_ref, b_ref, o_ref, cw, rsub):
    tm, d = o_ref.shape
    chunks = [slice(c * cw, (c + 1) * cw) for c in range(d // cw)]

    def body(r, carry):
        rows = pl.ds(pl.multiple_of(r * rsub, rsub), rsub)
        total = 0.0
        for cs in chunks:
            v = pre_norm(rows, cs)
            o_ref[rows, cs] = v
            total = total + jnp.sum(v, axis=-1, keepdims=True)
        mu = total / d
        sq = 0.0
        for cs in chunks:
            dev = o_ref[rows, cs] - mu
            sq = sq + jnp.sum(dev * dev, axis=-1, keepdims=True)
        inv = lax.rsqrt(sq / d + LN_EPS)
        for cs in chunks:
            o_ref[rows, cs] = (o_ref[rows, cs] - mu) * inv * g_ref[:, cs] + b_ref[:, cs]
        return carry

    lax.fori_loop(0, tm // rsub, body, 0)


def _cast_jobs(src_refs, dst_refs):
    for s_ref, d_ref in zip(src_refs, dst_refs):
        w = d_ref.shape[2]
        for c in range(d_ref.shape[0]):
            d_ref[c] = s_ref[0, :, c * w:(c + 1) * w].astype(BF16)


def _cast_job_specs(src, layer, cb, n_steps, step_of):
    _, r, c = src.shape
    rb = next(v for v in range(16, r + 1, 16) if r % v == 0 and r // v <= n_steps)
    nb = r // rb
    blk = lambda *g: jnp.minimum(step_of(*g), nb - 1)
    in_spec = pl.BlockSpec((1, rb, c), lambda *g: (layer, blk(*g), 0))
    out_spec = pl.BlockSpec((cb, rb, c // cb), lambda *g: (0, blk(*g), 0))
    return in_spec, out_spec, jax.ShapeDtypeStruct((cb, r, c // cb), BF16)


def _blocked_bf16(w, cb):
    r, c = w.shape
    return w.astype(BF16).reshape(r, cb, c // cb).transpose(1, 0, 2)


def _ffn_kernel(*refs, j, cw, rsub, n_jobs):
    z_ref, mod_ref, w1_ref, w3_ref, w2_ref, g_ref, b_ref = refs[:7]
    src_refs = refs[7:7 + n_jobs]
    o_ref = refs[7 + n_jobs]
    dst_refs = refs[8 + n_jobs:8 + 2 * n_jobs]
    h_ref = refs[8 + 2 * n_jobs]
    k = pl.program_id(1)
    d = o_ref.shape[1]
    chunks = [slice(c * cw, (c + 1) * cw) for c in range(d // cw)]

    @pl.when(k == 0)
    def _():
        for cs in chunks:
            shift = mod_ref[3 * j:3 * j + 1, cs]
            scale = mod_ref[3 * j + 1:3 * j + 2, cs]
            h_ref[:, cs] = (z_ref[:, cs] * (1.0 + scale) + shift).astype(BF16)
            o_ref[:, cs] = jnp.zeros((o_ref.shape[0], cw), F32)

    _cast_jobs(src_refs, dst_refs)
    h = h_ref[...]
    tf = w1_ref.shape[2]
    for kb in range(w1_ref.shape[0]):
        a = _dot(h, w1_ref[kb])
        b = _dot(h, w3_ref[kb])
        u = (a * jax.nn.sigmoid(a) * b).astype(BF16)

        for cs in chunks:
            o_ref[:, cs] += _dot(u, w2_ref[0, kb * tf:(kb + 1) * tf, cs])

    @pl.when(k == pl.num_programs(1) - 1)
    def _():
        def pre_norm(rows, cs):
            gate = mod_ref[3 * j + 2:3 * j + 3, cs]
            return ALPHA * z_ref[rows, cs] + (0.5 * gate) * o_ref[rows, cs]

        _residual_layer_norm(pre_norm, g_ref, b_ref, o_ref, cw, rsub)


def _ffn(z, mod, j, w1, w3, w2, ln_g, ln_b, tm, jobs=(), kb=1, cw=512):
    n, d = z.shape
    nk, _, tf = w1.shape
    nk //= kb
    step_of = lambda i, k: i * nk + k
    job_specs = [_cast_job_specs(src, layer, cb, (n // tm) * nk, step_of) for src, layer, cb in jobs]
    outs = pl.pallas_call(
        functools.partial(_ffn_kernel, j=j, cw=cw, rsub=min(tm, 256), n_jobs=len(jobs)),
        grid=(n // tm, nk),
        in_specs=[
            pl.BlockSpec((tm, d), lambda i, k: (i, 0)),
            pl.BlockSpec(mod.shape, lambda i, k: (0, 0)),
            pl.BlockSpec((kb, d, tf), lambda i, k: (k, 0, 0)),
            pl.BlockSpec((kb, d, tf), lambda i, k: (k, 0, 0)),
            pl.BlockSpec((1, kb * tf, d), lambda i, k: (0, k, 0)),
            pl.BlockSpec((1, d), lambda i, k: (0, 0)),
            pl.BlockSpec((1, d), lambda i, k: (0, 0)),
        ] + [js[0] for js in job_specs],
        out_specs=[pl.BlockSpec((tm, d), lambda i, k: (i, 0))] + [js[1] for js in job_specs],
        out_shape=[jax.ShapeDtypeStruct((n, d), F32)] + [js[2] for js in job_specs],
        scratch_shapes=[pltpu.VMEM((tm, d), BF16)],
        compiler_params=_cparams(("arbitrary", "arbitrary")),
        name="ffn",
    )(z, mod, w1, w3, w2, ln_g.reshape(1, d), ln_b.reshape(1, d), *[src for src, _, _ in jobs])
    return outs[0], list(outs[1:])


def _inproj_kernel(x_ref, mod_ref, w_ref, o_ref, h_ref, *, cw):
    @pl.when(pl.program_id(1) == 0)
    def _():
        for c in range(x_ref.shape[1] // cw):
            cs = slice(c * cw, (c + 1) * cw)
            h_ref[:, cs] = (x_ref[:, cs] * (1.0 + mod_ref[4:5, cs]) + mod_ref[3:4, cs]).astype(BF16)

    o_ref[...] = _dot(h_ref[...], w_ref[0]).astype(BF16)


def _inproj(x, mod, w_in, tm, cw=512):
    n, d = x.shape
    nb, _, tn = w_in.shape
    d_in = nb * tn
    return pl.pallas_call(
        functools.partial(_inproj_kernel, cw=cw),
        grid=(n // tm, nb),
        in_specs=[
            pl.BlockSpec((tm, d), lambda i, c: (i, 0)),
            pl.BlockSpec(mod.shape, lambda i, c: (0, 0)),
            pl.BlockSpec((1, d, tn), lambda i, c: (c, 0, 0)),
        ],
        out_specs=[pl.BlockSpec((tm, tn), lambda i, c: (i, c)), pl.BlockSpec((tm, d), lambda i, c: (i, 0))],
        out_shape=[jax.ShapeDtypeStruct((n, d_in), BF16), jax.ShapeDtypeStruct((n, d), BF16)],
        compiler_params=_cparams(("parallel", "arbitrary")),
        name="inproj",
    )(x, mod, w_in)


def _na_geometry(rows):
    assert rows % Q_ROWS == 0 and rows >= KW_ROWS and KW_ROWS % 2 == 0
    wr = min(WIN_R, rows)
    n_rb = rows // Q_ROWS
    q0 = np.arange(n_rb) * Q_ROWS
    kstart = np.clip(q0 - WIN_R // 2, 0, rows - KW_ROWS)
    offs = kstart - q0
    uniq = sorted(set(offs.tolist()), reverse=True)
    cfg = np.array([uniq.index(o) for o in offs.tolist()], np.int32)
    dr_cfg, ok_cfg = [], []
    for off in uniq:
        rb = int(np.nonzero(offs == off)[0][0])
        rq = q0[rb] + np.arange(Q_ROWS)
        rk = kstart[rb] + np.arange(KW_ROWS)
        r0 = np.clip(rq - wr // 2, 0, rows - wr)
        ok_cfg.append((rk[None, :] >= r0[:, None]) & (rk[None, :] < r0[:, None] + wr))
        dr_cfg.append(rk[None, :] - rq[:, None] + WIN_R - 1)
    return n_rb, kstart.astype(np.int32), cfg, np.stack(dr_cfg), np.stack(ok_cfg)


def _na_build_bias(h, rpb_ref, t_ref, bias_ref, dr_cfg, ok_cfg):
    n_dr, n_dc = 2 * WIN_R - 1, 2 * WIN_C - 1
    cq = lax.broadcasted_iota(jnp.int32, (GRID_W, 2 * GRID_W), 0)
    lane = lax.broadcasted_iota(jnp.int32, (GRID_W, 2 * GRID_W), 1)
    ck = lane & (GRID_W - 1)
    c0 = jnp.clip(cq - WIN_C // 2, 0, GRID_W - WIN_C)
    col_ok = (ck >= c0) & (ck < c0 + WIN_C)
    dc = jnp.clip(ck - cq, -(WIN_C - 1), WIN_C - 1) + WIN_C - 1
    base = h * (n_dr * n_dc)
    for dr in range(n_dr):
        acc = jnp.zeros((GRID_W, 2 * GRID_W), F32)
        for j in range(n_dc):
            acc = jnp.where(dc == j, rpb_ref[base + dr * n_dc + j], acc)
        t_ref[dr] = jnp.where(col_ok, acc * LOG2E, NEG_INF)
    left = lane < GRID_W
    neg = jnp.full((GRID_W, 2 * GRID_W), NEG_INF, F32)
    for c in range(dr_cfg.shape[0]):
        for rq in range(Q_ROWS):
            for rk in range(0, KW_ROWS, 2):
                even = t_ref[int(dr_cfg[c, rq, rk])] if ok_cfg[c, rq, rk] else neg
                odd = t_ref[int(dr_cfg[c, rq, rk + 1])] if ok_cfg[c, rq, rk + 1] else neg
                bias_ref[c, rq * GRID_W:(rq + 1) * GRID_W, rk * GRID_W:(rk + 2) * GRID_W] = jnp.where(left, even, odd)


def _na_kernel(cfg_ref, ks_ref, rpb_ref, q_ref, k_ref, v_ref, kc_ref, vc_ref, o_ref, t_ref, bias_ref,
               *, dr_cfg, ok_cfg):
    hg = pl.program_id(0)
    rb = pl.program_id(1)

    @pl.when(rb == 0)
    def _():
        for hh in range(NA_HEADS_PER_STEP):
            _na_build_bias(hg * NA_HEADS_PER_STEP + hh, rpb_ref, t_ref, bias_ref.at[hh], dr_cfg, ok_cfg)

    k0 = pl.multiple_of(ks_ref[rb] * GRID_W, GRID_W)
    cfg = cfg_ref[rb]
    scale = (HEAD_DIM ** -0.5) * LOG2E
    for hh in range(NA_HEADS_PER_STEP):
        hs = slice(hh * HEAD_DIM, (hh + 1) * HEAD_DIM)
        q = q_ref[:, hs]
        kwin = k_ref[pl.ds(k0, KW_ROWS * GRID_W), hs]
        vwin = v_ref[pl.ds(k0, KW_ROWS * GRID_W), hs]
        s_lat = _dot_nt(q, kwin) * scale + bias_ref[hh, cfg]
        s_ctx = _dot_nt(q, kc_ref[:, hs]) * scale
        m = jnp.maximum(jnp.max(s_lat, axis=-1, keepdims=True), jnp.max(s_ctx, axis=-1, keepdims=True))
        p_lat = jnp.exp2(s_lat - m)
        p_ctx = jnp.exp2(s_ctx - m)
        denom = jnp.sum(p_lat, axis=-1, keepdims=True) + jnp.sum(p_ctx, axis=-1, keepdims=True)
        o = _dot(p_lat.astype(BF16), vwin) + _dot(p_ctx.astype(BF16), vc_ref[:, hs])
        o_ref[:, hs] = (o / denom).astype(BF16)


def _na_attention(zin, zc, rpb_l):
    n = zin.shape[0]
    n_ctx = zc.shape[0]
    rows = n // GRID_W
    n_rb, kstart, cfg, dr_cfg, ok_cfg = _na_geometry(rows)
    tq = Q_ROWS * GRID_W
    hps = NA_HEADS_PER_STEP
    wb = hps * HEAD_DIM
    n_hg = N_HEADS // hps
    grid_spec = pltpu.PrefetchScalarGridSpec(
        num_scalar_prefetch=3,
        grid=(n_hg, n_rb),
        in_specs=[
            pl.BlockSpec((tq, wb), lambda g, r, *_: (r, g)),
            pl.BlockSpec((n, wb), lambda g, r, *_: (0, n_hg + g)),
            pl.BlockSpec((n, wb), lambda g, r, *_: (0, 2 * n_hg + g)),
            pl.BlockSpec((n_ctx, wb), lambda g, r, *_: (0, n_hg + g)),
            pl.BlockSpec((n_ctx, wb), lambda g, r, *_: (0, 2 * n_hg + g)),
        ],
        out_specs=pl.BlockSpec((tq, wb), lambda g, r, *_: (r, g)),
        scratch_shapes=[
            pltpu.VMEM((2 * WIN_R - 1, GRID_W, 2 * GRID_W), F32),
            pltpu.VMEM((hps, dr_cfg.shape[0], tq, KW_ROWS * GRID_W), F32),
        ],
    )
    return pl.pallas_call(
        functools.partial(_na_kernel, dr_cfg=dr_cfg, ok_cfg=ok_cfg),
        grid_spec=grid_spec,
        out_shape=jax.ShapeDtypeStruct((n, D_NA), BF16),
        compiler_params=_cparams(("arbitrary", "arbitrary")),
        name="na_attention",
    )(jnp.asarray(cfg), jnp.asarray(kstart), rpb_l.astype(F32).reshape(-1), zin, zin, zin, zc, zc)


def _ctx_attn_kernel(q_ref, k_ref, v_ref, o_ref):
    s = _dot_nt(q_ref[...], k_ref[...]) * (HEAD_DIM ** -0.5)
    p = jnp.exp(s - jnp.max(s, axis=-1, keepdims=True))
    o = _dot(p.astype(BF16), v_ref[...])
    o_ref[...] = (o / jnp.sum(p, axis=-1, keepdims=True)).astype(BF16)


def _ctx_attention(zc):
    n_ctx = zc.shape[0]
    hb = D_NA // HEAD_DIM
    return pl.pallas_call(
        _ctx_attn_kernel,
        grid=(N_HEADS,),
        in_specs=[
            pl.BlockSpec((n_ctx, HEAD_DIM), lambda h: (0, h)),
            pl.BlockSpec((n_ctx, HEAD_DIM), lambda h: (0, hb + h)),
            pl.BlockSpec((n_ctx, HEAD_DIM), lambda h: (0, 2 * hb + h)),
        ],
        out_specs=pl.BlockSpec((n_ctx, HEAD_DIM), lambda h: (0, h)),
        out_shape=jax.ShapeDtypeStruct((n_ctx, D_NA), BF16),
        compiler_params=_cparams(("parallel",)),
        name="ctx_attention",
    )(zc, zc, zc)


def _pow2_near(v):
    return 2.0 ** np.round(np.log2(v))


def _bf16_const(a):
    return jnp.asarray(a, F32).astype(BF16)


def _dft_consts(n1, n2):
    k = FOURIER_GROUP
    n_g = D_FOURIER // k
    n = n1 * n2
    sa, sb = 1.0 / _pow2_near(np.sqrt(k)), 1.0 / _pow2_near(np.sqrt(n2))
    jj = np.arange(k)
    ang = 2 * np.pi * ((jj[:, None] * jj[None, :]) % k) / k
    eye = np.eye(n_g)
    a_mat = np.concatenate([np.kron(eye, np.cos(ang)), -np.kron(eye, np.sin(ang))], axis=1) * sa
    ii = np.arange(n2)
    ang2 = 2 * np.pi * ((ii[:, None] * ii[None, :]) % n2) / n2
    c2, s2 = np.cos(ang2), np.sin(ang2)
    b_mat = np.block([[c2, s2], [-s2, c2]]) * sb
    m2 = np.arange(n2)[:, None, None]
    m1 = np.arange(n1)[None, :, None]
    nn1 = np.arange(n1)[None, None, :]
    ang3 = 2 * np.pi * ((nn1 * (n2 * m1 + m2)) % n) / n
    c_mat = np.concatenate([np.cos(ang3), np.sin(ang3)], axis=2)
    out_scale = float(1.0 / (np.sqrt(k * n) * sa * sb))
    return _bf16_const(a_mat), _bf16_const(b_mat), _bf16_const(c_mat), out_scale


def _fa_kernel(f_ref, a_ref, o_ref):
    z = _dot(f_ref[...], a_ref[...])
    o_ref[0] = z[:, :D_FOURIER].astype(BF16)
    o_ref[1] = z[:, D_FOURIER:].astype(BF16)


def _fb_kernel(z_ref, b_ref, o_ref):
    for j in range(z_ref.shape[1]):
        o_ref[:, j, :] = _dot(b_ref[...], z_ref[:, j, :]).astype(BF16)


def _fc_kernel(ur_ref, ui_ref, c_ref, o_ref, *, n1, mb, out_scale):
    for j in range(mb):
        cm = c_ref[j]
        res = _dot(cm[:, :n1], ur_ref[j]) + _dot(cm[:, n1:], ui_ref[j])
        o_ref[:, j, :] = res * out_scale


def _fourier_ctx_kernel(z_ref, c_ref, o_ref, *, n, out_scale):
    o_ref[...] = (_dot(c_ref[:, :n], z_ref[0]) + _dot(c_ref[:, n:], z_ref[1])) * out_scale


def _fourier_stage_a(zin, a_mat, tm):
    n = zin.shape[0]
    return pl.pallas_call(
        _fa_kernel,
        grid=(n // tm,),
        in_specs=[
            pl.BlockSpec((tm, D_FOURIER), lambda i: (i, COL_F)),
            pl.BlockSpec(a_mat.shape, lambda i: (0, 0)),
        ],
        out_specs=pl.BlockSpec((2, tm, D_FOURIER), lambda i: (0, i, 0)),
        out_shape=jax.ShapeDtypeStruct((2, n, D_FOURIER), BF16),
        compiler_params=_cparams(("parallel",)),
        name="fourier_a",
    )(zin, a_mat)


def _fourier_latent(zin, n1, n2, jt=16, mb=8):
    n = zin.shape[0]
    assert n == n1 * n2
    a_mat, b_mat, c_mat, out_scale = _dft_consts(n1, n2)
    zri = _fourier_stage_a(zin, a_mat, tm=min(n, 1024)).reshape(2 * n2, n1, D_FOURIER)
    jt = min(jt, n1)
    u = pl.pallas_call(
        _fb_kernel,
        grid=(n1 // jt,),
        in_specs=[
            pl.BlockSpec((2 * n2, jt, D_FOURIER), lambda c: (0, c, 0)),
            pl.BlockSpec(b_mat.shape, lambda c: (0, 0)),
        ],
        out_specs=pl.BlockSpec((2 * n2, jt, D_FOURIER), lambda c: (0, c, 0)),
        out_shape=jax.ShapeDtypeStruct((2 * n2, n1, D_FOURIER), BF16),
        compiler_params=_cparams(("parallel",)),
        name="fourier_b",
    )(zri, b_mat)
    nb = n2 // mb
    out = pl.pallas_call(
        functools.partial(_fc_kernel, n1=n1, mb=mb, out_scale=out_scale),
        grid=(nb,),
        in_specs=[
            pl.BlockSpec((mb, n1, D_FOURIER), lambda b: (b, 0, 0)),
            pl.BlockSpec((mb, n1, D_FOURIER), lambda b: (nb + b, 0, 0)),
            pl.BlockSpec((mb, n1, 2 * n1), lambda b: (b, 0, 0)),
        ],
        out_specs=pl.BlockSpec((n1, mb, D_FOURIER), lambda b: (0, b, 0)),
        out_shape=jax.ShapeDtypeStruct((n1, n2, D_FOURIER), F32),
        compiler_params=_cparams(("parallel",)),
        name="fourier_c",
    )(u, u, c_mat)
    return out.reshape(n, D_FOURIER)


def _fourier_ctx(zc):
    n = zc.shape[0]
    a_mat, _, _, out_scale = _dft_consts(n, 1)
    ii = np.arange(n)
    ang = 2 * np.pi * ((ii[:, None] * ii[None, :]) % n) / n
    c_mat = _bf16_const(np.concatenate([np.cos(ang), np.sin(ang)], axis=1))
    zri = _fourier_stage_a(zc, a_mat, tm=n)
    return pl.pallas_call(
        functools.partial(_fourier_ctx_kernel, n=n, out_scale=out_scale),
        grid=(1,),
        in_specs=[
            pl.BlockSpec((2, n, D_FOURIER), lambda i: (0, 0, 0)),
            pl.BlockSpec(c_mat.shape, lambda i: (0, 0)),
        ],
        out_specs=pl.BlockSpec((n, D_FOURIER), lambda i: (0, 0)),
        out_shape=jax.ShapeDtypeStruct((n, D_FOURIER), F32),
        compiler_params=_cparams(("arbitrary",)),
        name="fourier_ctx",
    )(zri, c_mat)


def _conv_kernel(ap_ref, ac_ref, an_ref, gp_ref, gc_ref, gn_ref, w_ref, b_ref, lg_ref, lb_ref, o_ref, win_ref,
                 rot_ref, *, tm, rs):
    i = pl.program_id(0)

    def glu(a_ref, g_ref):
        return a_ref[...].astype(F32) * jax.nn.sigmoid(g_ref[...].astype(F32))

    win_ref[0:CONV_HALO, :] = jnp.where(i == 0, 0.0, glu(ap_ref, gp_ref))
    win_ref[CONV_HALO:CONV_HALO + tm, :] = glu(ac_ref, gc_ref)
    win_ref[CONV_HALO + tm:, :] = jnp.where(i == pl.num_programs(0) - 1, 0.0, glu(an_ref, gn_ref))

    n_rot = rot_ref.shape[1]
    for s in range(8):
        rot_ref[s] = win_ref[s:s + n_rot, :]

    first = CONV_HALO - CONV_WIDTH // 2
    lanes = 128
    for r in range(tm // rs):
        parts = []
        for c in range(D_CONV // lanes):
            cs = slice(c * lanes, (c + 1) * lanes)
            acc = jnp.zeros((rs, lanes), F32)
            for t in range(CONV_WIDTH):
                off = r * rs + first + t
                base = off - off % 8
                acc = acc + rot_ref[off % 8, base:base + rs, cs] * w_ref[t:t + 1, cs]
            parts.append(acc)
        y = jnp.concatenate(parts, axis=-1) + b_ref[...]
        y = _layer_norm(y, lg_ref[...], lb_ref[...])
        o_ref[r * rs:(r + 1) * rs, :] = (y * jax.nn.sigmoid(y)).astype(BF16)


def _conv_module(zin, w_dw, b_dw, ln_g, ln_b, tm, rs=32):
    n = zin.shape[0]
    nt = n // tm
    hb = tm // CONV_HALO
    n_hb = n // CONV_HALO
    prev_map = lambda col: (lambda i: (jnp.maximum(i * hb - 1, 0), col))
    next_map = lambda col: (lambda i: (jnp.minimum((i + 1) * hb, n_hb - 1), col))
    cur_map = lambda col: (lambda i: (i, col))
    vec = lambda v: v.reshape(1, D_CONV).astype(F32)
    return pl.pallas_call(
        functools.partial(_conv_kernel, tm=tm, rs=rs),
        grid=(nt,),
        in_specs=[
            pl.BlockSpec((CONV_HALO, D_CONV), prev_map(COL_A)),
            pl.BlockSpec((tm, D_CONV), cur_map(COL_A)),
            pl.BlockSpec((CONV_HALO, D_CONV), next_map(COL_A)),
            pl.BlockSpec((CONV_HALO, D_CONV), prev_map(COL_G)),
            pl.BlockSpec((tm, D_CONV), cur_map(COL_G)),
            pl.BlockSpec((CONV_HALO, D_CONV), next_map(COL_G)),
            pl.BlockSpec((CONV_WIDTH, D_CONV), lambda i: (0, 0)),
            pl.BlockSpec((1, D_CONV), lambda i: (0, 0)),
            pl.BlockSpec((1, D_CONV), lambda i: (0, 0)),
            pl.BlockSpec((1, D_CONV), lambda i: (0, 0)),
        ],
        out_specs=pl.BlockSpec((tm, D_CONV), lambda i: (i, 0)),
        out_shape=jax.ShapeDtypeStruct((n, D_CONV), BF16),
        scratch_shapes=[pltpu.VMEM((tm + 2 * CONV_HALO, D_CONV), F32),
                        pltpu.VMEM((8, tm + 2 * CONV_HALO - 8, D_CONV), F32)],
        compiler_params=_cparams(("parallel",)),
        name="conv_module",
    )(zin, zin, zin, zin, zin, zin, w_dw.astype(F32), vec(b_dw), vec(ln_g), vec(ln_b))


def _gate_mix_kernel(h_ref, yna_ref, yf_ref, yc_ref, wg0_ref, wg1_ref, wg2_ref, bg0_ref, bg1_ref, bg2_ref,
                     wna_ref, wf_ref, wc_ref, m_ref):
    h = h_ref[...]
    g0 = jax.nn.sigmoid(_dot(h, wg0_ref[0]) + bg0_ref[...])
    m = g0 * _dot(yna_ref[...], wna_ref[0])
    g1 = jax.nn.sigmoid(_dot(h, wg1_ref[0]) + bg1_ref[...])
    m = m + g1 * _dot(yf_ref[...].astype(BF16), wf_ref[0])
    g2 = jax.nn.sigmoid(_dot(h, wg2_ref[0]) + bg2_ref[...])
    m = m + g2 * _dot(yc_ref[...], wc_ref[0])
    m_ref[...] = m.astype(BF16)


def _gate_mix(h, y_na, y_f, y_c, w_gate, b_gate, w_b_na, w_b_f, w_b_c, tm):
    n, d = h.shape
    nc, _, cw = w_b_na.shape
    ga = lambda br: (lambda c, i: (br * nc + c, 0, 0))
    gb = lambda br: (lambda c, i: (0, br * nc + c))
    pa = lambda c, i: (c, 0, 0)
    row = lambda c, i: (i, 0)
    bg = b_gate.reshape(1, -1)
    return pl.pallas_call(
        _gate_mix_kernel,
        grid=(nc, n // tm),
        in_specs=[
            pl.BlockSpec((tm, d), row),
            pl.BlockSpec((tm, D_NA), row),
            pl.BlockSpec((tm, D_FOURIER), row),
            pl.BlockSpec((tm, D_CONV), row),
            pl.BlockSpec((1, d, cw), ga(0)),
            pl.BlockSpec((1, d, cw), ga(1)),
            pl.BlockSpec((1, d, cw), ga(2)),
            pl.BlockSpec((1, cw), gb(0)),
            pl.BlockSpec((1, cw), gb(1)),
            pl.BlockSpec((1, cw), gb(2)),
            pl.BlockSpec((1, D_NA, cw), pa),
            pl.BlockSpec((1, D_FOURIER, cw), pa),
            pl.BlockSpec((1, D_CONV, cw), pa),
        ],
        out_specs=pl.BlockSpec((tm, cw), lambda c, i: (i, c)),
        out_shape=jax.ShapeDtypeStruct((n, d), BF16),
        compiler_params=_cparams(("parallel", "arbitrary")),
        name="gate_mix",
    )(h, y_na, y_f, y_c, w_gate, w_gate, w_gate, bg, bg, bg, w_b_na, w_b_f, w_b_c)


def _out_proj_kernel(m_ref, x_ref, mod_ref, wo_ref, g_ref, b_ref, o_ref):
    nc, _, cw = wo_ref.shape
    m = m_ref[...]
    for c in range(nc):
        cs = slice(c * cw, (c + 1) * cw)
        o_ref[:, cs] = ALPHA * x_ref[:, cs] + mod_ref[5:6, cs] * _dot(m, wo_ref[c])
    _residual_layer_norm(lambda rows, cs: o_ref[rows, cs], g_ref, b_ref, o_ref, cw, min(o_ref.shape[0], 256))


def _out_proj(m, x, mod, w_o, ln_g, ln_b, tm):
    n, d = x.shape
    row = lambda i: (i, 0)
    fixed = lambda i: (0, 0)
    return pl.pallas_call(
        _out_proj_kernel,
        grid=(n // tm,),
        in_specs=[
            pl.BlockSpec((tm, d), row),
            pl.BlockSpec((tm, d), row),
            pl.BlockSpec(mod.shape, fixed),
            pl.BlockSpec(w_o.shape, lambda i: (0, 0, 0)),
            pl.BlockSpec((1, d), fixed),
            pl.BlockSpec((1, d), fixed),
        ],
        out_specs=pl.BlockSpec((tm, d), row),
        out_shape=jax.ShapeDtypeStruct((n, d), F32),
        compiler_params=_cparams(("parallel",)),
        name="out_proj",
    )(m, x, mod, w_o, ln_g.reshape(1, d), ln_b.reshape(1, d))


def _merge(x, h, mod, y_na, y_f, y_c, w_gate, b_gate, w_b_na, w_b_f, w_b_c, w_o, ln_g, ln_b, tm_mix, tm_out):
    m = _gate_mix(h, y_na, y_f, y_c, w_gate, b_gate, w_b_na, w_b_f, w_b_c, tm=tm_mix)
    return _out_proj(m, x, mod, w_o, ln_g, ln_b, tm=tm_out)


def _row_tile(n, target):
    t = min(n, target)
    assert n % t == 0
    return t


def kernel(x, c, ctx, c_ctx, w_ada, b_ada, ffn1_w1, ffn1_w3, ffn1_w2, ffn2_w1, ffn2_w3, ffn2_w2, w_in, w_dw, b_dw, conv_ln_g, conv_ln_b, rpb, w_gate, b_gate, w_b_na, w_b_f, w_b_c, w_o, ln_g, ln_b):
    b_, n, d = x.shape
    assert b_ == 1 and c.shape[0] == 1
    n_ctx = ctx.shape[1]
    depth = w_ada.shape[0]

    c_rows = jnp.zeros((8, d), F32).at[0].set(c[0]).at[1].set(c_ctx)
    mods = _ada_mods(c_rows, w_ada, b_ada)

    tm = _row_tile(n, 1024)
    tmc = _row_tile(n_ctx, 256)
    nk = D_FF // FFN_TF
    n_in = D_IN // INPROJ_TN
    nc = d // MERGE_CW

    ffn_jobs = lambda w1, w3, w2, l: [(w1, l, nk), (w3, l, nk), (w2, l, 1)]
    mix_jobs = lambda l: [(w_in, l, n_in), (w_gate, l, 3 * nc), (w_b_na, l, nc), (w_b_f, l, nc), (w_b_c, l, nc),
                          (w_o, l, nc)]
    ffn1_w = [_blocked_bf16(ffn1_w1[0], nk), _blocked_bf16(ffn1_w3[0], nk), _blocked_bf16(ffn1_w2[0], 1)]

    xl = x[0]
    xc = ctx[0]
    for l in range(depth):
        last = l == depth - 1
        mod = mods[l, 0].reshape(N_MOD_USED, d)
        modc = mods[l, 1].reshape(N_MOD_USED, d)
        conv_w = (w_dw[l], b_dw[l], conv_ln_g[l], conv_ln_b[l])

        xl, cast = _ffn(xl, mod, 0, *ffn1_w, ln_g[l, 0], ln_b[l, 0], tm=tm,
                        jobs=mix_jobs(l) + ffn_jobs(ffn2_w1, ffn2_w3, ffn2_w2, l))
        w_in_l, wg_l, wna_l, wf_l, wc_l, wo_l = cast[:6]
        ffn2_w = cast[6:]
        xc, _ = _ffn(xc, modc, 0, *ffn1_w, ln_g[l, 0], ln_b[l, 0], tm=tmc, kb=CTX_FFN_KB)

        zin, hl = _inproj(xl, mod, w_in_l, tm=tm)
        zc, hc = _inproj(xc, modc, w_in_l, tm=tmc)

        merge_w = (wg_l, b_gate[l], wna_l, wf_l, wc_l, wo_l, ln_g[l, 1], ln_b[l, 1])
        y_na = _na_attention(zin, zc, rpb[l])
        y_f = _fourier_latent(zin, n1=n // GRID_W, n2=GRID_W)
        y_c = _conv_module(zin, *conv_w, tm=_row_tile(n, 256))
        xl_new = _merge(xl, hl, mod, y_na, y_f, y_c, *merge_w, tm_mix=tm, tm_out=_row_tile(n, 512))

        if not last:
            yc_na = _ctx_attention(zc)
            yc_f = _fourier_ctx(zc)
            yc_c = _conv_module(zc, *conv_w, tm=n_ctx)
            xc = _merge(xc, hc, modc, yc_na, yc_f, yc_c, *merge_w, tm_mix=tmc, tm_out=tmc)
        xl = xl_new

        jobs = [] if last else ffn_jobs(ffn1_w1, ffn1_w3, ffn1_w2, l + 1)
        xl, cast = _ffn(xl, mod, 1, *ffn2_w, ln_g[l, 2], ln_b[l, 2], tm=tm, jobs=jobs)
        if not last:
            ffn1_w = cast
            xc, _ = _ffn(xc, modc, 1, *ffn2_w, ln_g[l, 2], ln_b[l, 2], tm=tmc, kb=CTX_FFN_KB)
    return xl[None]
```

```python
import functools

import numpy as np
import jax
import jax.numpy as jnp
from jax import lax
from jax.experimental import pallas as pl
from jax.experimental.pallas import tpu as pltpu

F32 = jnp.float32
BF16 = jnp.bfloat16

D_MODEL = 2048
DEPTH = 2
GRID_W = 64
N_HEADS = 8
HEAD_DIM = 128
D_NA = N_HEADS * HEAD_DIM
WIN_R = 8
WIN_C = 16
FOURIER_GROUP = 128
D_FOURIER = 512
D_CONV = 512
CONV_WIDTH = 31
D_FF = 5632
D_IN = 3 * D_NA + D_FOURIER + 2 * D_CONV
N_MOD_USED = 6
ALPHA = (2 * DEPTH) ** 0.25
LN_EPS = 1e-5
NEG_INF = -1e30

COL_F = 3 * D_NA // D_FOURIER
COL_A = COL_F + 1
COL_G = COL_F + 2

LOG2E = float(np.log2(np.e))
NA_HEADS_PER_STEP = 4
Q_ROWS = 4
KW_ROWS = 12
CONV_HALO = 16
FFN_TF = 256
FFN_TF_NO_JOBS = 512
CTX_FFN_BYTES = 2 * 1024 * 1024
ADA_RK = 128
INPROJ_TN = 1536
MERGE_CW = 512

VMEM_LIMIT_V7X = 60 * 1024 * 1024


def _cparams(sem):
    return pltpu.CompilerParams(dimension_semantics=sem, vmem_limit_bytes=VMEM_LIMIT_V7X)


def _layer_norm(r, g, b):
    mu = jnp.mean(r, axis=-1, keepdims=True)
    d = r - mu
    var = jnp.mean(d * d, axis=-1, keepdims=True)
    return d * lax.rsqrt(var + LN_EPS) * g + b


def _dot(a, b):
    return jnp.dot(a, b, preferred_element_type=F32)


def _dot_nt(a, b):
    return lax.dot_general(a, b, (((1,), (1,)), ((), ())), preferred_element_type=F32)


def _ada_kernel(c_ref, w_ref, b_ref, o_ref):
    @pl.when(pl.program_id(1) == 0)
    def _():
        o_ref[0] = jnp.broadcast_to(b_ref[0], o_ref.shape[1:])

    c = c_ref[...]
    s = (c * jax.nn.sigmoid(c)).astype(BF16)
    o_ref[0] += _dot(s, w_ref[0].astype(BF16))


def _ada_mods(c_rows, w_ada, b_ada, rk=ADA_RK):
    n_l, d, _ = w_ada.shape
    n_out = N_MOD_USED * d
    return pl.pallas_call(
        _ada_kernel,
        grid=(n_l, d // rk),
        in_specs=[
            pl.BlockSpec((8, rk), lambda l, k: (0, k)),
            pl.BlockSpec((1, rk, n_out), lambda l, k: (l, k, 0)),
            pl.BlockSpec((1, 1, n_out), lambda l, k: (l, 0, 0)),
        ],
        out_specs=pl.BlockSpec((1, 8, n_out), lambda l, k: (l, 0, 0)),
        out_shape=jax.ShapeDtypeStruct((n_l, 8, n_out), F32),
        compiler_params=_cparams(("parallel", "arbitrary")),
        name="ada_mods",
    )(c_rows, w_ada, b_ada.reshape(n_l, 1, -1))


def _residual_layer_norm(pre_norm, g_ref, b_ref, o_ref, cw, rsub):
    tm, d = o_ref.shape
    chunks = [slice(c * cw, (c + 1) * cw) for c in range(d // cw)]

    def body(r, carry):
        rows = pl.ds(pl.multiple_of(r * rsub, rsub), rsub)
        total = 0.0
        for cs in chunks:
            v = pre_norm(rows, cs)
            o_ref[rows, cs] = v
            total = total + jnp.sum(v, axis=-1, keepdims=True)
        mu = total / d
        sq = 0.0
        for cs in chunks:
            dev = o_ref[rows, cs] - mu
            sq = sq + jnp.sum(dev * dev, axis=-1, keepdims=True)
        inv = lax.rsqrt(sq / d + LN_EPS)
        for cs in chunks:
            o_ref[rows, cs] = (o_ref[rows, cs] - mu) * inv * g_ref[:, cs] + b_ref[:, cs]
        return carry

    lax.fori_loop(0, tm // rsub, body, 0)


def _cast_jobs(src_refs, dst_refs):
    for s_ref, d_ref in zip(src_refs, dst_refs):
        w = d_ref.shape[2]
        for c in range(d_ref.shape[0]):
            d_ref[c] = s_ref[0, :, c * w:(c + 1) * w].astype(BF16)


def _cast_job_specs(src, layer, cb, n_steps, step_of):
    _, r, c = src.shape
    rb = next(v for v in range(16, r + 1, 16) if r % v == 0 and r // v <= n_steps)
    nb = r // rb
    blk = lambda *g: jnp.minimum(step_of(*g), nb - 1)
    in_spec = pl.BlockSpec((1, rb, c), lambda *g: (layer, blk(*g), 0))
    out_spec = pl.BlockSpec((cb, rb, c // cb), lambda *g: (0, blk(*g), 0))
    return in_spec, out_spec, jax.ShapeDtypeStruct((cb, r, c // cb), BF16)


def _blocked_bf16(w, cb):
    r, c = w.shape
    return w.astype(BF16).reshape(r, cb, c // cb).transpose(1, 0, 2)


def _ffn_kernel(*refs, j, cw, rsub, n_jobs):
    z_ref, mod_ref, w1_ref, w3_ref, w2_ref, g_ref, b_ref = refs[:7]
    src_refs = refs[7:7 + n_jobs]
    o_ref = refs[7 + n_jobs]
    dst_refs = refs[8 + n_jobs:8 + 2 * n_jobs]
    h_ref = refs[8 + 2 * n_jobs]
    k = pl.program_id(1)
    d = o_ref.shape[1]
    chunks = [slice(c * cw, (c + 1) * cw) for c in range(d // cw)]

    @pl.when(k == 0)
    def _():
        for cs in chunks:
            shift = mod_ref[3 * j:3 * j + 1, cs]
            scale = mod_ref[3 * j + 1:3 * j + 2, cs]
            h_ref[:, cs] = (z_ref[:, cs] * (1.0 + scale) + shift).astype(BF16)
            o_ref[:, cs] = jnp.zeros((o_ref.shape[0], cw), F32)

    _cast_jobs(src_refs, dst_refs)
    h = h_ref[...]
    tf = w1_ref.shape[2]
    for kb in range(w1_ref.shape[0]):
        a = _dot(h, w1_ref[kb])
        b = _dot(h, w3_ref[kb])
        u = (a * jax.nn.sigmoid(a) * b).astype(BF16)

        for cs in chunks:
            o_ref[:, cs] += _dot(u, w2_ref[0, kb * tf:(kb + 1) * tf, cs])

    @pl.when(k == pl.num_programs(1) - 1)
    def _():
        def pre_norm(rows, cs):
            gate = mod_ref[3 * j + 2:3 * j + 3, cs]
            return ALPHA * z_ref[rows, cs] + (0.5 * gate) * o_ref[rows, cs]

        _residual_layer_norm(pre_norm, g_ref, b_ref, o_ref, cw, rsub)


def _ffn(z, mod, j, w1, w3, w2, ln_g, ln_b, tm, jobs=(), kb=1, cw=512):
    n, d = z.shape
    nk, _, tf = w1.shape
    nk //= kb
    step_of = lambda i, k: i * nk + k
    job_specs = [_cast_job_specs(src, layer, cb, (n // tm) * nk, step_of) for src, layer, cb in jobs]
    outs = pl.pallas_call(
        functools.partial(_ffn_kernel, j=j, cw=cw, rsub=min(tm, 256), n_jobs=len(jobs)),
        grid=(n // tm, nk),
        in_specs=[
            pl.BlockSpec((tm, d), lambda i, k: (i, 0)),
            pl.BlockSpec(mod.shape, lambda i, k: (0, 0)),
            pl.BlockSpec((kb, d, tf), lambda i, k: (k, 0, 0)),
            pl.BlockSpec((kb, d, tf), lambda i, k: (k, 0, 0)),
            pl.BlockSpec((1, kb * tf, d), lambda i, k: (0, k, 0)),
            pl.BlockSpec((1, d), lambda i, k: (0, 0)),
            pl.BlockSpec((1, d), lambda i, k: (0, 0)),
        ] + [js[0] for js in job_specs],
        out_specs=[pl.BlockSpec((tm, d), lambda i, k: (i, 0))] + [js[1] for js in job_specs],
        out_shape=[jax.ShapeDtypeStruct((n, d), F32)] + [js[2] for js in job_specs],
        scratch_shapes=[pltpu.VMEM((tm, d), BF16)],
        compiler_params=_cparams(("arbitrary", "arbitrary")),
        name="ffn",
    )(z, mod, w1, w3, w2, ln_g.reshape(1, d), ln_b.reshape(1, d), *[src for src, _, _ in jobs])
    return outs[0], list(outs[1:])


def _inproj_kernel(x_ref, mod_ref, w_ref, o_ref, h_ref, *, cw):
    @pl.when(pl.program_id(1) == 0)
    def _():
        for c in range(x_ref.shape[1] // cw):
            cs = slice(c * cw, (c + 1) * cw)
            h_ref[:, cs] = (x_ref[:, cs] * (1.0 + mod_ref[4:5, cs]) + mod_ref[3:4, cs]).astype(BF16)

    o_ref[...] = _dot(h_ref[...], w_ref[pl.program_id(1)]).astype(BF16)


def _inproj(x, mod, w_in, tm, cw=512):
    n, d = x.shape
    nb, _, tn = w_in.shape
    d_in = nb * tn
    return pl.pallas_call(
        functools.partial(_inproj_kernel, cw=cw),
        grid=(n // tm, nb),
        in_specs=[
            pl.BlockSpec((tm, d), lambda i, c: (i, 0)),
            pl.BlockSpec(mod.shape, lambda i, c: (0, 0)),
            pl.BlockSpec((nb, d, tn), lambda i, c: (0, 0, 0), pipeline_mode=pl.Buffered(1)),
        ],
        out_specs=[pl.BlockSpec((tm, tn), lambda i, c: (i, c)), pl.BlockSpec((tm, d), lambda i, c: (i, 0))],
        out_shape=[jax.ShapeDtypeStruct((n, d_in), BF16), jax.ShapeDtypeStruct((n, d), BF16)],
        compiler_params=_cparams(("parallel", "arbitrary")),
        name="inproj",
    )(x, mod, w_in)


def _na_geometry(rows):
    assert rows % Q_ROWS == 0 and rows >= KW_ROWS and KW_ROWS % 2 == 0
    wr = min(WIN_R, rows)
    n_rb = rows // Q_ROWS
    q0 = np.arange(n_rb) * Q_ROWS
    kstart = np.clip(q0 - WIN_R // 2, 0, rows - KW_ROWS)
    offs = kstart - q0
    uniq = sorted(set(offs.tolist()), reverse=True)
    cfg = np.array([uniq.index(o) for o in offs.tolist()], np.int32)
    dr_cfg, ok_cfg = [], []
    for off in uniq:
        rb = int(np.nonzero(offs == off)[0][0])
        rq = q0[rb] + np.arange(Q_ROWS)
        rk = kstart[rb] + np.arange(KW_ROWS)
        r0 = np.clip(rq - wr // 2, 0, rows - wr)
        ok_cfg.append((rk[None, :] >= r0[:, None]) & (rk[None, :] < r0[:, None] + wr))
        dr_cfg.append(rk[None, :] - rq[:, None] + WIN_R - 1)
    return n_rb, kstart.astype(np.int32), cfg, np.stack(dr_cfg), np.stack(ok_cfg)


def _na_build_bias(h, rpb_ref, t_ref, bias_ref, dr_cfg, ok_cfg):
    n_dr, n_dc = 2 * WIN_R - 1, 2 * WIN_C - 1
    cq = lax.broadcasted_iota(jnp.int32, (GRID_W, 2 * GRID_W), 0)
    lane = lax.broadcasted_iota(jnp.int32, (GRID_W, 2 * GRID_W), 1)
    ck = lane & (GRID_W - 1)
    c0 = jnp.clip(cq - WIN_C // 2, 0, GRID_W - WIN_C)
    col_ok = (ck >= c0) & (ck < c0 + WIN_C)
    dc = jnp.clip(ck - cq, -(WIN_C - 1), WIN_C - 1) + WIN_C - 1
    base = h * (n_dr * n_dc)
    for dr in range(n_dr):
        acc = jnp.zeros((GRID_W, 2 * GRID_W), F32)
        for j in range(n_dc):
            acc = jnp.where(dc == j, rpb_ref[base + dr * n_dc + j], acc)
        t_ref[dr] = jnp.where(col_ok, acc * LOG2E, NEG_INF)
    left = lane < GRID_W
    neg = jnp.full((GRID_W, 2 * GRID_W), NEG_INF, F32)
    for c in range(dr_cfg.shape[0]):
        for rq in range(Q_ROWS):
            for rk in range(0, KW_ROWS, 2):
                even = t_ref[int(dr_cfg[c, rq, rk])] if ok_cfg[c, rq, rk] else neg
                odd = t_ref[int(dr_cfg[c, rq, rk + 1])] if ok_cfg[c, rq, rk + 1] else neg
                bias_ref[c, rq * GRID_W:(rq + 1) * GRID_W, rk * GRID_W:(rk + 2) * GRID_W] = jnp.where(left, even, odd)


def _na_kernel(cfg_ref, ks_ref, rpb_ref, q_ref, k_ref, v_ref, kc_ref, vc_ref, o_ref, t_ref, bias_ref,
               *, dr_cfg, ok_cfg):
    hg = pl.program_id(0)
    rb = pl.program_id(1)

    @pl.when(rb == 0)
    def _():
        for hh in range(NA_HEADS_PER_STEP):
            _na_build_bias(hg * NA_HEADS_PER_STEP + hh, rpb_ref, t_ref, bias_ref.at[hh], dr_cfg, ok_cfg)

    k0 = pl.multiple_of(ks_ref[rb] * GRID_W, GRID_W)
    cfg = cfg_ref[rb]
    scale = (HEAD_DIM ** -0.5) * LOG2E
    for hh in range(NA_HEADS_PER_STEP):
        hs = slice(hh * HEAD_DIM, (hh + 1) * HEAD_DIM)
        q = q_ref[:, hs]
        kwin = k_ref[pl.ds(k0, KW_ROWS * GRID_W), hs]
        vwin = v_ref[pl.ds(k0, KW_ROWS * GRID_W), hs]
        s_lat = _dot_nt(q, kwin) * scale + bias_ref[hh, cfg]
        s_ctx = _dot_nt(q, kc_ref[:, hs]) * scale
        m = jnp.maximum(jnp.max(s_lat, axis=-1, keepdims=True), jnp.max(s_ctx, axis=-1, keepdims=True))
        p_lat = jnp.exp2(s_lat - m)
        p_ctx = jnp.exp2(s_ctx - m)
        denom = jnp.sum(p_lat, axis=-1, keepdims=True) + jnp.sum(p_ctx, axis=-1, keepdims=True)
        o = _dot(p_lat.astype(BF16), vwin) + _dot(p_ctx.astype(BF16), vc_ref[:, hs])
        o_ref[:, hs] = (o / denom).astype(BF16)


def _na_attention(zin, zc, rpb_l):
    n = zin.shape[0]
    n_ctx = zc.shape[0]
    rows = n // GRID_W
    n_rb, kstart, cfg, dr_cfg, ok_cfg = _na_geometry(rows)
    tq = Q_ROWS * GRID_W
    hps = NA_HEADS_PER_STEP
    wb = hps * HEAD_DIM
    n_hg = N_HEADS // hps
    grid_spec = pltpu.PrefetchScalarGridSpec(
        num_scalar_prefetch=3,
        grid=(n_hg, n_rb),
        in_specs=[
            pl.BlockSpec((tq, wb), lambda g, r, *_: (r, g)),
            pl.BlockSpec((n, wb), lambda g, r, *_: (0, n_hg + g)),
            pl.BlockSpec((n, wb), lambda g, r, *_: (0, 2 * n_hg + g)),
            pl.BlockSpec((n_ctx, wb), lambda g, r, *_: (0, n_hg + g)),
            pl.BlockSpec((n_ctx, wb), lambda g, r, *_: (0, 2 * n_hg + g)),
        ],
        out_specs=pl.BlockSpec((tq, wb), lambda g, r, *_: (r, g)),
        scratch_shapes=[
            pltpu.VMEM((2 * WIN_R - 1, GRID_W, 2 * GRID_W), F32),
            pltpu.VMEM((hps, dr_cfg.shape[0], tq, KW_ROWS * GRID_W), F32),
        ],
    )
    return pl.pallas_call(
        functools.partial(_na_kernel, dr_cfg=dr_cfg, ok_cfg=ok_cfg),
        grid_spec=grid_spec,
        out_shape=jax.ShapeDtypeStruct((n, D_NA), BF16),
        compiler_params=_cparams(("arbitrary", "arbitrary")),
        name="na_attention",
    )(jnp.asarray(cfg), jnp.asarray(kstart), rpb_l.astype(F32).reshape(-1), zin, zin, zin, zc, zc)


def _ctx_attn_kernel(q_ref, k_ref, v_ref, o_ref):
    s = _dot_nt(q_ref[...], k_ref[...]) * (HEAD_DIM ** -0.5)
    p = jnp.exp(s - jnp.max(s, axis=-1, keepdims=True))
    o = _dot(p.astype(BF16), v_ref[...])
    o_ref[...] = (o / jnp.sum(p, axis=-1, keepdims=True)).astype(BF16)


def _ctx_attention(zc):
    n_ctx = zc.shape[0]
    hb = D_NA // HEAD_DIM
    return pl.pallas_call(
        _ctx_attn_kernel,
        grid=(N_HEADS,),
        in_specs=[
            pl.BlockSpec((n_ctx, HEAD_DIM), lambda h: (0, h)),
            pl.BlockSpec((n_ctx, HEAD_DIM), lambda h: (0, hb + h)),
            pl.BlockSpec((n_ctx, HEAD_DIM), lambda h: (0, 2 * hb + h)),
        ],
        out_specs=pl.BlockSpec((n_ctx, HEAD_DIM), lambda h: (0, h)),
        out_shape=jax.ShapeDtypeStruct((n_ctx, D_NA), BF16),
        compiler_params=_cparams(("parallel",)),
        name="ctx_attention",
    )(zc, zc, zc)


def _pow2_near(v):
    return 2.0 ** np.round(np.log2(v))


def _bf16_const(a):
    return jnp.asarray(a, F32).astype(BF16)


def _dft_consts(n1, n2):
    k = FOURIER_GROUP
    n_g = D_FOURIER // k
    n = n1 * n2
    sa, sb = 1.0 / _pow2_near(np.sqrt(k)), 1.0 / _pow2_near(np.sqrt(n2))
    jj = np.arange(k)
    ang = 2 * np.pi * ((jj[:, None] * jj[None, :]) % k) / k
    eye = np.eye(n_g)
    a_mat = np.concatenate([np.kron(eye, np.cos(ang)), -np.kron(eye, np.sin(ang))], axis=1) * sa
    ii = np.arange(n2)
    ang2 = 2 * np.pi * ((ii[:, None] * ii[None, :]) % n2) / n2
    c2, s2 = np.cos(ang2), np.sin(ang2)
    b_mat = np.block([[c2, s2], [-s2, c2]]) * sb
    m2 = np.arange(n2)[:, None, None]
    m1 = np.arange(n1)[None, :, None]
    nn1 = np.arange(n1)[None, None, :]
    ang3 = 2 * np.pi * ((nn1 * (n2 * m1 + m2)) % n) / n
    c_mat = np.concatenate([np.cos(ang3), np.sin(ang3)], axis=2)
    out_scale = float(1.0 / (np.sqrt(k * n) * sa * sb))
    return _bf16_const(a_mat), _bf16_const(b_mat), _bf16_const(c_mat), out_scale


def _fa_kernel(f_ref, a_ref, o_ref):
    z = _dot(f_ref[...], a_ref[...])
    o_ref[0] = z[:, :D_FOURIER].astype(BF16)
    o_ref[1] = z[:, D_FOURIER:].astype(BF16)


def _fb_kernel(z_ref, b_ref, o_ref):
    for j in range(z_ref.shape[1]):
        o_ref[:, j, :] = _dot(b_ref[...], z_ref[:, j, :]).astype(BF16)


def _fc_kernel(ur_ref, ui_ref, c_ref, o_ref, *, n1, mb, out_scale):
    for j in range(mb):
        cm = c_ref[j]
        res = _dot(cm[:, :n1], ur_ref[j]) + _dot(cm[:, n1:], ui_ref[j])
        o_ref[:, j, :] = res * out_scale


def _fourier_ctx_kernel(z_ref, c_ref, o_ref, *, n, out_scale):
    o_ref[...] = (_dot(c_ref[:, :n], z_ref[0]) + _dot(c_ref[:, n:], z_ref[1])) * out_scale


def _fourier_stage_a(zin, a_mat, tm):
    n = zin.shape[0]
    return pl.pallas_call(
        _fa_kernel,
        grid=(n // tm,),
        in_specs=[
            pl.BlockSpec((tm, D_FOURIER), lambda i: (i, COL_F)),
            pl.BlockSpec(a_mat.shape, lambda i: (0, 0)),
        ],
        out_specs=pl.BlockSpec((2, tm, D_FOURIER), lambda i: (0, i, 0)),
        out_shape=jax.ShapeDtypeStruct((2, n, D_FOURIER), BF16),
        compiler_params=_cparams(("parallel",)),
        name="fourier_a",
    )(zin, a_mat)


def _fourier_latent(zin, n1, n2, jt=16, mb=8):
    n = zin.shape[0]
    assert n == n1 * n2
    a_mat, b_mat, c_mat, out_scale = _dft_consts(n1, n2)
    zri = _fourier_stage_a(zin, a_mat, tm=min(n, 1024)).reshape(2 * n2, n1, D_FOURIER)
    jt = min(jt, n1)
    u = pl.pallas_call(
        _fb_kernel,
        grid=(n1 // jt,),
        in_specs=[
            pl.BlockSpec((2 * n2, jt, D_FOURIER), lambda c: (0, c, 0)),
            pl.BlockSpec(b_mat.shape, lambda c: (0, 0)),
        ],
        out_specs=pl.BlockSpec((2 * n2, jt, D_FOURIER), lambda c: (0, c, 0)),
        out_shape=jax.ShapeDtypeStruct((2 * n2, n1, D_FOURIER), BF16),
        compiler_params=_cparams(("parallel",)),
        name="fourier_b",
    )(zri, b_mat)
    nb = n2 // mb
    out = pl.pallas_call(
        functools.partial(_fc_kernel, n1=n1, mb=mb, out_scale=out_scale),
        grid=(nb,),
        in_specs=[
            pl.BlockSpec((mb, n1, D_FOURIER), lambda b: (b, 0, 0)),
            pl.BlockSpec((mb, n1, D_FOURIER), lambda b: (nb + b, 0, 0)),
            pl.BlockSpec((mb, n1, 2 * n1), lambda b: (b, 0, 0)),
        ],
        out_specs=pl.BlockSpec((n1, mb, D_FOURIER), lambda b: (0, b, 0)),
        out_shape=jax.ShapeDtypeStruct((n1, n2, D_FOURIER), F32),
        compiler_params=_cparams(("parallel",)),
        name="fourier_c",
    )(u, u, c_mat)
    return out.reshape(n, D_FOURIER)


def _fourier_ctx(zc):
    n = zc.shape[0]
    a_mat, _, _, out_scale = _dft_consts(n, 1)
    ii = np.arange(n)
    ang = 2 * np.pi * ((ii[:, None] * ii[None, :]) % n) / n
    c_mat = _bf16_const(np.concatenate([np.cos(ang), np.sin(ang)], axis=1))
    zri = _fourier_stage_a(zc, a_mat, tm=n)
    return pl.pallas_call(
        functools.partial(_fourier_ctx_kernel, n=n, out_scale=out_scale),
        grid=(1,),
        in_specs=[
            pl.BlockSpec((2, n, D_FOURIER), lambda i: (0, 0, 0)),
            pl.BlockSpec(c_mat.shape, lambda i: (0, 0)),
        ],
        out_specs=pl.BlockSpec((n, D_FOURIER), lambda i: (0, 0)),
        out_shape=jax.ShapeDtypeStruct((n, D_FOURIER), F32),
        compiler_params=_cparams(("arbitrary",)),
        name="fourier_ctx",
    )(zri, c_mat)


def _conv_kernel(ap_ref, ac_ref, an_ref, gp_ref, gc_ref, gn_ref, w_ref, b_ref, lg_ref, lb_ref, o_ref, win_ref,
                 rot_ref, *, tm, rs):
    i = pl.program_id(0)

    def glu(a_ref, g_ref):
        return a_ref[...].astype(F32) * jax.nn.sigmoid(g_ref[...].astype(F32))

    win_ref[0:CONV_HALO, :] = jnp.where(i == 0, 0.0, glu(ap_ref, gp_ref))
    win_ref[CONV_HALO:CONV_HALO + tm, :] = glu(ac_ref, gc_ref)
    win_ref[CONV_HALO + tm:, :] = jnp.where(i == pl.num_programs(0) - 1, 0.0, glu(an_ref, gn_ref))

    n_rot = rot_ref.shape[1]
    for s in range(8):
        rot_ref[s] = win_ref[s:s + n_rot, :]

    first = CONV_HALO - CONV_WIDTH // 2
    lanes = 128
    for r in range(tm // rs):
        parts = []
        for c in range(D_CONV // lanes):
            cs = slice(c * lanes, (c + 1) * lanes)
            acc = jnp.zeros((rs, lanes), F32)
            for t in range(CONV_WIDTH):
                off = r * rs + first + t
                base = off - off % 8
                acc = acc + rot_ref[off % 8, base:base + rs, cs] * w_ref[t:t + 1, cs]
            parts.append(acc)
        y = jnp.concatenate(parts, axis=-1) + b_ref[...]
        y = _layer_norm(y, lg_ref[...], lb_ref[...])
        o_ref[r * rs:(r + 1) * rs, :] = (y * jax.nn.sigmoid(y)).astype(BF16)


def _conv_module(zin, w_dw, b_dw, ln_g, ln_b, tm, rs=32):
    n = zin.shape[0]
    nt = n // tm
    hb = tm // CONV_HALO
    n_hb = n // CONV_HALO
    prev_map = lambda col: (lambda i: (jnp.maximum(i * hb - 1, 0), col))
    next_map = lambda col: (lambda i: (jnp.minimum((i + 1) * hb, n_hb - 1), col))
    cur_map = lambda col: (lambda i: (i, col))
    vec = lambda v: v.reshape(1, D_CONV).astype(F32)
    return pl.pallas_call(
        functools.partial(_conv_kernel, tm=tm, rs=rs),
        grid=(nt,),
        in_specs=[
            pl.BlockSpec((CONV_HALO, D_CONV), prev_map(COL_A)),
            pl.BlockSpec((tm, D_CONV), cur_map(COL_A)),
            pl.BlockSpec((CONV_HALO, D_CONV), next_map(COL_A)),
            pl.BlockSpec((CONV_HALO, D_CONV), prev_map(COL_G)),
            pl.BlockSpec((tm, D_CONV), cur_map(COL_G)),
            pl.BlockSpec((CONV_HALO, D_CONV), next_map(COL_G)),
            pl.BlockSpec((CONV_WIDTH, D_CONV), lambda i: (0, 0)),
            pl.BlockSpec((1, D_CONV), lambda i: (0, 0)),
            pl.BlockSpec((1, D_CONV), lambda i: (0, 0)),
            pl.BlockSpec((1, D_CONV), lambda i: (0, 0)),
        ],
        out_specs=pl.BlockSpec((tm, D_CONV), lambda i: (i, 0)),
        out_shape=jax.ShapeDtypeStruct((n, D_CONV), BF16),
        scratch_shapes=[pltpu.VMEM((tm + 2 * CONV_HALO, D_CONV), F32),
                        pltpu.VMEM((8, tm + 2 * CONV_HALO - 8, D_CONV), F32)],
        compiler_params=_cparams(("parallel",)),
        name="conv_module",
    )(zin, zin, zin, zin, zin, zin, w_dw.astype(F32), vec(b_dw), vec(ln_g), vec(ln_b))


def _gate_mix_kernel(h_ref, yna_ref, yf_ref, yc_ref, wg0_ref, wg1_ref, wg2_ref, bg0_ref, bg1_ref, bg2_ref,
                     wna_ref, wf_ref, wc_ref, m_ref):
    h = h_ref[...]
    g0 = jax.nn.sigmoid(_dot(h, wg0_ref[0]) + bg0_ref[...])
    m = g0 * _dot(yna_ref[...], wna_ref[0])
    g1 = jax.nn.sigmoid(_dot(h, wg1_ref[0]) + bg1_ref[...])
    m = m + g1 * _dot(yf_ref[...].astype(BF16), wf_ref[0])
    g2 = jax.nn.sigmoid(_dot(h, wg2_ref[0]) + bg2_ref[...])
    m = m + g2 * _dot(yc_ref[...], wc_ref[0])
    m_ref[...] = m.astype(BF16)


def _gate_mix(h, y_na, y_f, y_c, w_gate, b_gate, w_b_na, w_b_f, w_b_c, tm):
    n, d = h.shape
    nc, _, cw = w_b_na.shape
    ga = lambda br: (lambda c, i: (br * nc + c, 0, 0))
    gb = lambda br: (lambda c, i: (0, br * nc + c))
    pa = lambda c, i: (c, 0, 0)
    row = lambda c, i: (i, 0)
    bg = b_gate.reshape(1, -1)
    return pl.pallas_call(
        _gate_mix_kernel,
        grid=(nc, n // tm),
        in_specs=[
            pl.BlockSpec((tm, d), row),
            pl.BlockSpec((tm, D_NA), row),
            pl.BlockSpec((tm, D_FOURIER), row),
            pl.BlockSpec((tm, D_CONV), row),
            pl.BlockSpec((1, d, cw), ga(0)),
            pl.BlockSpec((1, d, cw), ga(1)),
            pl.BlockSpec((1, d, cw), ga(2)),
            pl.BlockSpec((1, cw), gb(0)),
            pl.BlockSpec((1, cw), gb(1)),
            pl.BlockSpec((1, cw), gb(2)),
            pl.BlockSpec((1, D_NA, cw), pa),
            pl.BlockSpec((1, D_FOURIER, cw), pa),
            pl.BlockSpec((1, D_CONV, cw), pa),
        ],
        out_specs=pl.BlockSpec((tm, cw), lambda c, i: (i, c)),
        out_shape=jax.ShapeDtypeStruct((n, d), BF16),
        compiler_params=_cparams(("parallel", "arbitrary")),
        name="gate_mix",
    )(h, y_na, y_f, y_c, w_gate, w_gate, w_gate, bg, bg, bg, w_b_na, w_b_f, w_b_c)


def _out_proj_kernel(m_ref, x_ref, mod_ref, wo_ref, g_ref, b_ref, o_ref):
    nc, _, cw = wo_ref.shape
    m = m_ref[...]
    for c in range(nc):
        cs = slice(c * cw, (c + 1) * cw)
        o_ref[:, cs] = ALPHA * x_ref[:, cs] + mod_ref[5:6, cs] * _dot(m, wo_ref[c])
    _residual_layer_norm(lambda rows, cs: o_ref[rows, cs], g_ref, b_ref, o_ref, cw, min(o_ref.shape[0], 256))


def _out_proj(m, x, mod, w_o, ln_g, ln_b, tm):
    n, d = x.shape
    row = lambda i: (i, 0)
    fixed = lambda i: (0, 0)
    return pl.pallas_call(
        _out_proj_kernel,
        grid=(n // tm,),
        in_specs=[
            pl.BlockSpec((tm, d), row),
            pl.BlockSpec((tm, d), row),
            pl.BlockSpec(mod.shape, fixed),
            pl.BlockSpec(w_o.shape, lambda i: (0, 0, 0)),
            pl.BlockSpec((1, d), fixed),
            pl.BlockSpec((1, d), fixed),
        ],
        out_specs=pl.BlockSpec((tm, d), row),
        out_shape=jax.ShapeDtypeStruct((n, d), F32),
        compiler_params=_cparams(("parallel",)),
        name="out_proj",
    )(m, x, mod, w_o, ln_g.reshape(1, d), ln_b.reshape(1, d))


def _merge(x, h, mod, y_na, y_f, y_c, w_gate, b_gate, w_b_na, w_b_f, w_b_c, w_o, ln_g, ln_b, tm_mix, tm_out):
    m = _gate_mix(h, y_na, y_f, y_c, w_gate, b_gate, w_b_na, w_b_f, w_b_c, tm=tm_mix)
    return _out_proj(m, x, mod, w_o, ln_g, ln_b, tm=tm_out)


def _row_tile(n, target):
    t = min(n, target)
    assert n % t == 0
    return t


def kernel(x, c, ctx, c_ctx, w_ada, b_ada, ffn1_w1, ffn1_w3, ffn1_w2, ffn2_w1, ffn2_w3, ffn2_w2, w_in, w_dw, b_dw, conv_ln_g, conv_ln_b, rpb, w_gate, b_gate, w_b_na, w_b_f, w_b_c, w_o, ln_g, ln_b):
    b_, n, d = x.shape
    assert b_ == 1 and c.shape[0] == 1
    n_ctx = ctx.shape[1]
    depth = w_ada.shape[0]

    c_rows = jnp.zeros((8, d), F32).at[0].set(c[0]).at[1].set(c_ctx)
    mods = _ada_mods(c_rows, w_ada, b_ada)

    tm = _row_tile(n, 1024)
    tmc = _row_tile(n_ctx, 256)
    nk1 = D_FF // FFN_TF
    ffn2_tf = lambda l: FFN_TF_NO_JOBS if l == depth - 1 else FFN_TF
    ctx_kb = lambda tf: max(1, CTX_FFN_BYTES // (d * tf * 2))
    n_in = D_IN // INPROJ_TN
    nc = d // MERGE_CW

    ffn_jobs = lambda w1, w3, w2, l, nk: [(w1, l, nk), (w3, l, nk), (w2, l, 1)]
    mix_jobs = lambda l: [(w_in, l, n_in), (w_gate, l, 3 * nc), (w_b_na, l, nc), (w_b_f, l, nc), (w_b_c, l, nc),
                          (w_o, l, nc)]
    ffn1_w = [_blocked_bf16(ffn1_w1[0], nk1), _blocked_bf16(ffn1_w3[0], nk1), _blocked_bf16(ffn1_w2[0], 1)]

    xl = x[0]
    xc = ctx[0]
    for l in range(depth):
        last = l == depth - 1
        mod = mods[l, 0].reshape(N_MOD_USED, d)
        modc = mods[l, 1].reshape(N_MOD_USED, d)
        conv_w = (w_dw[l], b_dw[l], conv_ln_g[l], conv_ln_b[l])

        xl, cast = _ffn(xl, mod, 0, *ffn1_w, ln_g[l, 0], ln_b[l, 0], tm=tm,
                        jobs=mix_jobs(l) + ffn_jobs(ffn2_w1, ffn2_w3, ffn2_w2, l, D_FF // ffn2_tf(l)))
        w_in_l, wg_l, wna_l, wf_l, wc_l, wo_l = cast[:6]
        ffn2_w = cast[6:]
        xc, _ = _ffn(xc, modc, 0, *ffn1_w, ln_g[l, 0], ln_b[l, 0], tm=tmc, kb=ctx_kb(FFN_TF))

        zin, hl = _inproj(xl, mod, w_in_l, tm=tm)
        zc, hc = _inproj(xc, modc, w_in_l, tm=tmc)

        merge_w = (wg_l, b_gate[l], wna_l, wf_l, wc_l, wo_l, ln_g[l, 1], ln_b[l, 1])
        y_na = _na_attention(zin, zc, rpb[l])
        y_f = _fourier_latent(zin, n1=n // GRID_W, n2=GRID_W)
        y_c = _conv_module(zin, *conv_w, tm=_row_tile(n, 512))
        xl_new = _merge(xl, hl, mod, y_na, y_f, y_c, *merge_w, tm_mix=tm, tm_out=_row_tile(n, 512))

        if not last:
            yc_na = _ctx_attention(zc)
            yc_f = _fourier_ctx(zc)
            yc_c = _conv_module(zc, *conv_w, tm=n_ctx)
            xc = _merge(xc, hc, modc, yc_na, yc_f, yc_c, *merge_w, tm_mix=tmc, tm_out=tmc)
        xl = xl_new

        jobs = [] if last else ffn_jobs(ffn1_w1, ffn1_w3, ffn1_w2, l + 1, nk1)
        xl, cast = _ffn(xl, mod, 1, *ffn2_w, ln_g[l, 2], ln_b[l, 2], tm=tm, jobs=jobs)
        if not last:
            ffn1_w = cast
            xc, _ = _ffn(xc, modc, 1, *ffn2_w, ln_g[l, 2], ln_b[l, 2], tm=tmc, kb=ctx_kb(ffn2_tf(l)))
    return xl[None]
```

```python
import functools

import numpy as np
import jax
import jax.numpy as jnp
from jax import lax
from jax.experimental import pallas as pl
from jax.experimental.pallas import tpu as pltpu

F32 = jnp.float32
BF16 = jnp.bfloat16

D_MODEL = 2048
DEPTH = 2
GRID_W = 64
N_HEADS = 8
HEAD_DIM = 128
D_NA = N_HEADS * HEAD_DIM
WIN_R = 8
WIN_C = 16
FOURIER_GROUP = 128
D_FOURIER = 512
D_CONV = 512
CONV_WIDTH = 31
D_FF = 5632
D_IN = 3 * D_NA + D_FOURIER + 2 * D_CONV
N_MOD_USED = 6
ALPHA = (2 * DEPTH) ** 0.25
LN_EPS = 1e-5
NEG_INF = -1e30

COL_F = 3 * D_NA // D_FOURIER
COL_A = COL_F + 1
COL_G = COL_F + 2

LOG2E = float(np.log2(np.e))
NA_HEADS_PER_STEP = 4
Q_ROWS = 4
KW_ROWS = 12
CONV_HALO = 16
FFN1_TF = 256
FFN2_TF = 512
CTX_FFN_BYTES = 2 * 1024 * 1024
ADA_RK = 128
INPROJ_TN = 1536
MERGE_CW = 512

VMEM_LIMIT_V7X = 60 * 1024 * 1024


def _cparams(sem):
    return pltpu.CompilerParams(dimension_semantics=sem, vmem_limit_bytes=VMEM_LIMIT_V7X)


def _layer_norm(r, g, b):
    mu = jnp.mean(r, axis=-1, keepdims=True)
    d = r - mu
    var = jnp.mean(d * d, axis=-1, keepdims=True)
    return d * lax.rsqrt(var + LN_EPS) * g + b


def _dot(a, b):
    return jnp.dot(a, b, preferred_element_type=F32)


def _dot_nt(a, b):
    return lax.dot_general(a, b, (((1,), (1,)), ((), ())), preferred_element_type=F32)


def _ada_kernel(c_ref, w_ref, b_ref, o_ref):
    @pl.when(pl.program_id(1) == 0)
    def _():
        o_ref[0] = jnp.broadcast_to(b_ref[0], o_ref.shape[1:])

    c = c_ref[...]
    s = (c * jax.nn.sigmoid(c)).astype(BF16)
    o_ref[0] += _dot(s, w_ref[0].astype(BF16))


def _ada_mods(c_rows, w_ada, b_ada, rk=ADA_RK):
    n_l, d, _ = w_ada.shape
    n_out = N_MOD_USED * d
    return pl.pallas_call(
        _ada_kernel,
        grid=(n_l, d // rk),
        in_specs=[
            pl.BlockSpec((8, rk), lambda l, k: (0, k)),
            pl.BlockSpec((1, rk, n_out), lambda l, k: (l, k, 0)),
            pl.BlockSpec((1, 1, n_out), lambda l, k: (l, 0, 0)),
        ],
        out_specs=pl.BlockSpec((1, 8, n_out), lambda l, k: (l, 0, 0)),
        out_shape=jax.ShapeDtypeStruct((n_l, 8, n_out), F32),
        compiler_params=_cparams(("parallel", "arbitrary")),
        name="ada_mods",
    )(c_rows, w_ada, b_ada.reshape(n_l, 1, -1))


def _residual_layer_norm(pre_norm, g_ref, b_ref, o_ref, cw, rsub):
    tm, d = o_ref.shape
    chunks = [slice(c * cw, (c + 1) * cw) for c in range(d // cw)]

    def body(r, carry):
        rows = pl.ds(pl.multiple_of(r * rsub, rsub), rsub)
        total = 0.0
        for cs in chunks:
            v = pre_norm(rows, cs)
            o_ref[rows, cs] = v
            total = total + jnp.sum(v, axis=-1, keepdims=True)
        mu = total / d
        sq = 0.0
        for cs in chunks:
            dev = o_ref[rows, cs] - mu
            sq = sq + jnp.sum(dev * dev, axis=-1, keepdims=True)
        inv = lax.rsqrt(sq / d + LN_EPS)
        for cs in chunks:
            o_ref[rows, cs] = (o_ref[rows, cs] - mu) * inv * g_ref[:, cs] + b_ref[:, cs]
        return carry

    lax.fori_loop(0, tm // rsub, body, 0)


def _cast_jobs(src_refs, dst_refs):
    for s_ref, d_ref in zip(src_refs, dst_refs):
        w = d_ref.shape[2]
        for c in range(d_ref.shape[0]):
            d_ref[c] = s_ref[0, :, c * w:(c + 1) * w].astype(BF16)


def _cast_job_specs(src, layer, cb, n_steps, step_of):
    _, r, c = src.shape
    rb = next(v for v in range(16, r + 1, 16) if r % v == 0 and r // v <= n_steps)
    nb = r // rb
    blk = lambda *g: jnp.minimum(step_of(*g), nb - 1)
    in_spec = pl.BlockSpec((1, rb, c), lambda *g: (layer, blk(*g), 0))
    out_spec = pl.BlockSpec((cb, rb, c // cb), lambda *g: (0, blk(*g), 0))
    return in_spec, out_spec, jax.ShapeDtypeStruct((cb, r, c // cb), BF16)


def _blocked_bf16(w, cb):
    r, c = w.shape
    return w.astype(BF16).reshape(r, cb, c // cb).transpose(1, 0, 2)


def _ffn_kernel(*refs, j, cw, rsub, n_jobs):
    z_ref, mod_ref, w1_ref, w3_ref, w2_ref, g_ref, b_ref = refs[:7]
    src_refs = refs[7:7 + n_jobs]
    o_ref = refs[7 + n_jobs]
    dst_refs = refs[8 + n_jobs:8 + 2 * n_jobs]
    h_ref = refs[8 + 2 * n_jobs]
    k = pl.program_id(1)
    d = o_ref.shape[1]
    chunks = [slice(c * cw, (c + 1) * cw) for c in range(d // cw)]

    @pl.when(k == 0)
    def _():
        for cs in chunks:
            shift = mod_ref[3 * j:3 * j + 1, cs]
            scale = mod_ref[3 * j + 1:3 * j + 2, cs]
            h_ref[:, cs] = (z_ref[:, cs] * (1.0 + scale) + shift).astype(BF16)
            o_ref[:, cs] = jnp.zeros((o_ref.shape[0], cw), F32)

    _cast_jobs(src_refs, dst_refs)
    h = h_ref[...]
    tf = w1_ref.shape[2]
    for kb in range(w1_ref.shape[0]):
        a = _dot(h, w1_ref[kb])
        b = _dot(h, w3_ref[kb])
        u = (a * jax.nn.sigmoid(a) * b).astype(BF16)

        for cs in chunks:
            o_ref[:, cs] += _dot(u, w2_ref[0, kb * tf:(kb + 1) * tf, cs])

    @pl.when(k == pl.num_programs(1) - 1)
    def _():
        def pre_norm(rows, cs):
            gate = mod_ref[3 * j + 2:3 * j + 3, cs]
            return ALPHA * z_ref[rows, cs] + (0.5 * gate) * o_ref[rows, cs]

        _residual_layer_norm(pre_norm, g_ref, b_ref, o_ref, cw, rsub)


def _ffn(z, mod, j, w1, w3, w2, ln_g, ln_b, tm, jobs=(), kb=1, cw=512):
    n, d = z.shape
    nk, _, tf = w1.shape
    nk //= kb
    step_of = lambda i, k: i * nk + k
    job_specs = [_cast_job_specs(src, layer, cb, (n // tm) * nk, step_of) for src, layer, cb in jobs]
    outs = pl.pallas_call(
        functools.partial(_ffn_kernel, j=j, cw=cw, rsub=min(tm, 256), n_jobs=len(jobs)),
        grid=(n // tm, nk),
        in_specs=[
            pl.BlockSpec((tm, d), lambda i, k: (i, 0)),
            pl.BlockSpec(mod.shape, lambda i, k: (0, 0)),
            pl.BlockSpec((kb, d, tf), lambda i, k: (k, 0, 0)),
            pl.BlockSpec((kb, d, tf), lambda i, k: (k, 0, 0)),
            pl.BlockSpec((1, kb * tf, d), lambda i, k: (0, k, 0)),
            pl.BlockSpec((1, d), lambda i, k: (0, 0)),
            pl.BlockSpec((1, d), lambda i, k: (0, 0)),
        ] + [js[0] for js in job_specs],
        out_specs=[pl.BlockSpec((tm, d), lambda i, k: (i, 0))] + [js[1] for js in job_specs],
        out_shape=[jax.ShapeDtypeStruct((n, d), F32)] + [js[2] for js in job_specs],
        scratch_shapes=[pltpu.VMEM((tm, d), BF16)],
        compiler_params=_cparams(("arbitrary", "arbitrary")),
        name="ffn",
    )(z, mod, w1, w3, w2, ln_g.reshape(1, d), ln_b.reshape(1, d), *[src for src, _, _ in jobs])
    return outs[0], list(outs[1:])


def _inproj_kernel(x_ref, mod_ref, w_ref, o_ref, h_ref, *, cw):
    @pl.when(pl.program_id(1) == 0)
    def _():
        for c in range(x_ref.shape[1] // cw):
            cs = slice(c * cw, (c + 1) * cw)
            h_ref[:, cs] = (x_ref[:, cs] * (1.0 + mod_ref[4:5, cs]) + mod_ref[3:4, cs]).astype(BF16)

    o_ref[...] = _dot(h_ref[...], w_ref[pl.program_id(1)]).astype(BF16)


def _inproj(x, mod, w_in, tm, cw=512):
    n, d = x.shape
    nb, _, tn = w_in.shape
    d_in = nb * tn
    return pl.pallas_call(
        functools.partial(_inproj_kernel, cw=cw),
        grid=(n // tm, nb),
        in_specs=[
            pl.BlockSpec((tm, d), lambda i, c: (i, 0)),
            pl.BlockSpec(mod.shape, lambda i, c: (0, 0)),
            pl.BlockSpec((nb, d, tn), lambda i, c: (0, 0, 0), pipeline_mode=pl.Buffered(1)),
        ],
        out_specs=[pl.BlockSpec((tm, tn), lambda i, c: (i, c)), pl.BlockSpec((tm, d), lambda i, c: (i, 0))],
        out_shape=[jax.ShapeDtypeStruct((n, d_in), BF16), jax.ShapeDtypeStruct((n, d), BF16)],
        compiler_params=_cparams(("parallel", "arbitrary")),
        name="inproj",
    )(x, mod, w_in)


def _na_geometry(rows):
    assert rows % Q_ROWS == 0 and rows >= KW_ROWS and KW_ROWS % 2 == 0
    wr = min(WIN_R, rows)
    n_rb = rows // Q_ROWS
    q0 = np.arange(n_rb) * Q_ROWS
    kstart = np.clip(q0 - WIN_R // 2, 0, rows - KW_ROWS)
    offs = kstart - q0
    uniq = sorted(set(offs.tolist()), reverse=True)
    cfg = np.array([uniq.index(o) for o in offs.tolist()], np.int32)
    dr_cfg, ok_cfg = [], []
    for off in uniq:
        rb = int(np.nonzero(offs == off)[0][0])
        rq = q0[rb] + np.arange(Q_ROWS)
        rk = kstart[rb] + np.arange(KW_ROWS)
        r0 = np.clip(rq - wr // 2, 0, rows - wr)
        ok_cfg.append((rk[None, :] >= r0[:, None]) & (rk[None, :] < r0[:, None] + wr))
        dr_cfg.append(rk[None, :] - rq[:, None] + WIN_R - 1)
    return n_rb, kstart.astype(np.int32), cfg, np.stack(dr_cfg), np.stack(ok_cfg)


def _na_build_bias(h, rpb_ref, t_ref, bias_ref, dr_cfg, ok_cfg):
    n_dr, n_dc = 2 * WIN_R - 1, 2 * WIN_C - 1
    cq = lax.broadcasted_iota(jnp.int32, (GRID_W, 2 * GRID_W), 0)
    lane = lax.broadcasted_iota(jnp.int32, (GRID_W, 2 * GRID_W), 1)
    ck = lane & (GRID_W - 1)
    c0 = jnp.clip(cq - WIN_C // 2, 0, GRID_W - WIN_C)
    col_ok = (ck >= c0) & (ck < c0 + WIN_C)
    dc = jnp.clip(ck - cq, -(WIN_C - 1), WIN_C - 1) + WIN_C - 1
    base = h * (n_dr * n_dc)
    for dr in range(n_dr):
        acc = jnp.zeros((GRID_W, 2 * GRID_W), F32)
        for j in range(n_dc):
            acc = jnp.where(dc == j, rpb_ref[base + dr * n_dc + j], acc)
        t_ref[dr] = jnp.where(col_ok, acc * LOG2E, NEG_INF)
    left = lane < GRID_W
    neg = jnp.full((GRID_W, 2 * GRID_W), NEG_INF, F32)
    for c in range(dr_cfg.shape[0]):
        for rq in range(Q_ROWS):
            for rk in range(0, KW_ROWS, 2):
                even = t_ref[int(dr_cfg[c, rq, rk])] if ok_cfg[c, rq, rk] else neg
                odd = t_ref[int(dr_cfg[c, rq, rk + 1])] if ok_cfg[c, rq, rk + 1] else neg
                bias_ref[c, rq * GRID_W:(rq + 1) * GRID_W, rk * GRID_W:(rk + 2) * GRID_W] = jnp.where(left, even, odd)


def _na_kernel(cfg_ref, ks_ref, rpb_ref, q_ref, k_ref, v_ref, kc_ref, vc_ref, o_ref, t_ref, bias_ref,
               *, dr_cfg, ok_cfg):
    hg = pl.program_id(0)
    rb = pl.program_id(1)

    @pl.when(rb == 0)
    def _():
        for hh in range(NA_HEADS_PER_STEP):
            _na_build_bias(hg * NA_HEADS_PER_STEP + hh, rpb_ref, t_ref, bias_ref.at[hh], dr_cfg, ok_cfg)

    k0 = pl.multiple_of(ks_ref[rb] * GRID_W, GRID_W)
    cfg = cfg_ref[rb]
    scale = (HEAD_DIM ** -0.5) * LOG2E
    for hh in range(NA_HEADS_PER_STEP):
        hs = slice(hh * HEAD_DIM, (hh + 1) * HEAD_DIM)
        q = q_ref[:, hs]
        kwin = k_ref[pl.ds(k0, KW_ROWS * GRID_W), hs]
        vwin = v_ref[pl.ds(k0, KW_ROWS * GRID_W), hs]
        s_lat = _dot_nt(q, kwin) * scale + bias_ref[hh, cfg]
        s_ctx = _dot_nt(q, kc_ref[:, hs]) * scale
        m = jnp.maximum(jnp.max(s_lat, axis=-1, keepdims=True), jnp.max(s_ctx, axis=-1, keepdims=True))
        p_lat = jnp.exp2(s_lat - m)
        p_ctx = jnp.exp2(s_ctx - m)
        denom = jnp.sum(p_lat, axis=-1, keepdims=True) + jnp.sum(p_ctx, axis=-1, keepdims=True)
        o = _dot(p_lat.astype(BF16), vwin) + _dot(p_ctx.astype(BF16), vc_ref[:, hs])
        o_ref[:, hs] = (o / denom).astype(BF16)


def _na_attention(zin, zc, rpb_l):
    n = zin.shape[0]
    n_ctx = zc.shape[0]
    rows = n // GRID_W
    n_rb, kstart, cfg, dr_cfg, ok_cfg = _na_geometry(rows)
    tq = Q_ROWS * GRID_W
    hps = NA_HEADS_PER_STEP
    wb = hps * HEAD_DIM
    n_hg = N_HEADS // hps
    grid_spec = pltpu.PrefetchScalarGridSpec(
        num_scalar_prefetch=3,
        grid=(n_hg, n_rb),
        in_specs=[
            pl.BlockSpec((tq, wb), lambda g, r, *_: (r, g)),
            pl.BlockSpec((n, wb), lambda g, r, *_: (0, n_hg + g)),
            pl.BlockSpec((n, wb), lambda g, r, *_: (0, 2 * n_hg + g)),
            pl.BlockSpec((n_ctx, wb), lambda g, r, *_: (0, n_hg + g)),
            pl.BlockSpec((n_ctx, wb), lambda g, r, *_: (0, 2 * n_hg + g)),
        ],
        out_specs=pl.BlockSpec((tq, wb), lambda g, r, *_: (r, g)),
        scratch_shapes=[
            pltpu.VMEM((2 * WIN_R - 1, GRID_W, 2 * GRID_W), F32),
            pltpu.VMEM((hps, dr_cfg.shape[0], tq, KW_ROWS * GRID_W), F32),
        ],
    )
    return pl.pallas_call(
        functools.partial(_na_kernel, dr_cfg=dr_cfg, ok_cfg=ok_cfg),
        grid_spec=grid_spec,
        out_shape=jax.ShapeDtypeStruct((n, D_NA), BF16),
        compiler_params=_cparams(("arbitrary", "arbitrary")),
        name="na_attention",
    )(jnp.asarray(cfg), jnp.asarray(kstart), rpb_l.astype(F32).reshape(-1), zin, zin, zin, zc, zc)


def _ctx_attn_kernel(q_ref, k_ref, v_ref, o_ref):
    s = _dot_nt(q_ref[...], k_ref[...]) * (HEAD_DIM ** -0.5)
    p = jnp.exp(s - jnp.max(s, axis=-1, keepdims=True))
    o = _dot(p.astype(BF16), v_ref[...])
    o_ref[...] = (o / jnp.sum(p, axis=-1, keepdims=True)).astype(BF16)


def _ctx_attention(zc):
    n_ctx = zc.shape[0]
    hb = D_NA // HEAD_DIM
    return pl.pallas_call(
        _ctx_attn_kernel,
        grid=(N_HEADS,),
        in_specs=[
            pl.BlockSpec((n_ctx, HEAD_DIM), lambda h: (0, h)),
            pl.BlockSpec((n_ctx, HEAD_DIM), lambda h: (0, hb + h)),
            pl.BlockSpec((n_ctx, HEAD_DIM), lambda h: (0, 2 * hb + h)),
        ],
        out_specs=pl.BlockSpec((n_ctx, HEAD_DIM), lambda h: (0, h)),
        out_shape=jax.ShapeDtypeStruct((n_ctx, D_NA), BF16),
        compiler_params=_cparams(("parallel",)),
        name="ctx_attention",
    )(zc, zc, zc)


def _pow2_near(v):
    return 2.0 ** np.round(np.log2(v))


def _bf16_const(a):
    return jnp.asarray(a, F32).astype(BF16)


def _dft_consts(n1, n2):
    k = FOURIER_GROUP
    n_g = D_FOURIER // k
    n = n1 * n2
    sa, sb = 1.0 / _pow2_near(np.sqrt(k)), 1.0 / _pow2_near(np.sqrt(n2))
    jj = np.arange(k)
    ang = 2 * np.pi * ((jj[:, None] * jj[None, :]) % k) / k
    eye = np.eye(n_g)
    a_mat = np.concatenate([np.kron(eye, np.cos(ang)), -np.kron(eye, np.sin(ang))], axis=1) * sa
    ii = np.arange(n2)
    ang2 = 2 * np.pi * ((ii[:, None] * ii[None, :]) % n2) / n2
    c2, s2 = np.cos(ang2), np.sin(ang2)
    b_mat = np.block([[c2, s2], [-s2, c2]]) * sb
    m2 = np.arange(n2)[:, None, None]
    m1 = np.arange(n1)[None, :, None]
    nn1 = np.arange(n1)[None, None, :]
    ang3 = 2 * np.pi * ((nn1 * (n2 * m1 + m2)) % n) / n
    c_mat = np.concatenate([np.cos(ang3), np.sin(ang3)], axis=2)
    out_scale = float(1.0 / (np.sqrt(k * n) * sa * sb))
    return _bf16_const(a_mat), _bf16_const(b_mat), _bf16_const(c_mat), out_scale


def _fa_kernel(f_ref, a_ref, o_ref):
    z = _dot(f_ref[...], a_ref[...])
    o_ref[0] = z[:, :D_FOURIER].astype(BF16)
    o_ref[1] = z[:, D_FOURIER:].astype(BF16)


def _fb_kernel(z_ref, b_ref, o_ref):
    for j in range(z_ref.shape[1]):
        o_ref[:, j, :] = _dot(b_ref[...], z_ref[:, j, :]).astype(BF16)


def _fc_kernel(ur_ref, ui_ref, c_ref, o_ref, *, n1, mb, out_scale):
    for j in range(mb):
        cm = c_ref[j]
        res = _dot(cm[:, :n1], ur_ref[j]) + _dot(cm[:, n1:], ui_ref[j])
        o_ref[:, j, :] = res * out_scale


def _fourier_ctx_kernel(z_ref, c_ref, o_ref, *, n, out_scale):
    o_ref[...] = (_dot(c_ref[:, :n], z_ref[0]) + _dot(c_ref[:, n:], z_ref[1])) * out_scale


def _fourier_stage_a(zin, a_mat, tm):
    n = zin.shape[0]
    return pl.pallas_call(
        _fa_kernel,
        grid=(n // tm,),
        in_specs=[
            pl.BlockSpec((tm, D_FOURIER), lambda i: (i, COL_F)),
            pl.BlockSpec(a_mat.shape, lambda i: (0, 0)),
        ],
        out_specs=pl.BlockSpec((2, tm, D_FOURIER), lambda i: (0, i, 0)),
        out_shape=jax.ShapeDtypeStruct((2, n, D_FOURIER), BF16),
        compiler_params=_cparams(("parallel",)),
        name="fourier_a",
    )(zin, a_mat)


def _fourier_latent(zin, n1, n2, jt=16, mb=8):
    n = zin.shape[0]
    assert n == n1 * n2
    a_mat, b_mat, c_mat, out_scale = _dft_consts(n1, n2)
    zri = _fourier_stage_a(zin, a_mat, tm=min(n, 1024)).reshape(2 * n2, n1, D_FOURIER)
    jt = min(jt, n1)
    u = pl.pallas_call(
        _fb_kernel,
        grid=(n1 // jt,),
        in_specs=[
            pl.BlockSpec((2 * n2, jt, D_FOURIER), lambda c: (0, c, 0)),
            pl.BlockSpec(b_mat.shape, lambda c: (0, 0)),
        ],
        out_specs=pl.BlockSpec((2 * n2, jt, D_FOURIER), lambda c: (0, c, 0)),
        out_shape=jax.ShapeDtypeStruct((2 * n2, n1, D_FOURIER), BF16),
        compiler_params=_cparams(("parallel",)),
        name="fourier_b",
    )(zri, b_mat)
    nb = n2 // mb
    out = pl.pallas_call(
        functools.partial(_fc_kernel, n1=n1, mb=mb, out_scale=out_scale),
        grid=(nb,),
        in_specs=[
            pl.BlockSpec((mb, n1, D_FOURIER), lambda b: (b, 0, 0)),
            pl.BlockSpec((mb, n1, D_FOURIER), lambda b: (nb + b, 0, 0)),
            pl.BlockSpec((mb, n1, 2 * n1), lambda b: (b, 0, 0)),
        ],
        out_specs=pl.BlockSpec((n1, mb, D_FOURIER), lambda b: (0, b, 0)),
        out_shape=jax.ShapeDtypeStruct((n1, n2, D_FOURIER), F32),
        compiler_params=_cparams(("parallel",)),
        name="fourier_c",
    )(u, u, c_mat)
    return out.reshape(n, D_FOURIER)


def _fourier_ctx(zc):
    n = zc.shape[0]
    a_mat, _, _, out_scale = _dft_consts(n, 1)
    ii = np.arange(n)
    ang = 2 * np.pi * ((ii[:, None] * ii[None, :]) % n) / n
    c_mat = _bf16_const(np.concatenate([np.cos(ang), np.sin(ang)], axis=1))
    zri = _fourier_stage_a(zc, a_mat, tm=n)
    return pl.pallas_call(
        functools.partial(_fourier_ctx_kernel, n=n, out_scale=out_scale),
        grid=(1,),
        in_specs=[
            pl.BlockSpec((2, n, D_FOURIER), lambda i: (0, 0, 0)),
            pl.BlockSpec(c_mat.shape, lambda i: (0, 0)),
        ],
        out_specs=pl.BlockSpec((n, D_FOURIER), lambda i: (0, 0)),
        out_shape=jax.ShapeDtypeStruct((n, D_FOURIER), F32),
        compiler_params=_cparams(("arbitrary",)),
        name="fourier_ctx",
    )(zri, c_mat)


def _conv_kernel(ap_ref, ac_ref, an_ref, gp_ref, gc_ref, gn_ref, w_ref, b_ref, lg_ref, lb_ref, o_ref, win_ref,
                 rot_ref, *, tm, rs):
    i = pl.program_id(0)

    def glu(a_ref, g_ref):
        return a_ref[...].astype(F32) * jax.nn.sigmoid(g_ref[...].astype(F32))

    win_ref[0:CONV_HALO, :] = jnp.where(i == 0, 0.0, glu(ap_ref, gp_ref))
    win_ref[CONV_HALO:CONV_HALO + tm, :] = glu(ac_ref, gc_ref)
    win_ref[CONV_HALO + tm:, :] = jnp.where(i == pl.num_programs(0) - 1, 0.0, glu(an_ref, gn_ref))

    n_rot = rot_ref.shape[1]
    for s in range(8):
        rot_ref[s] = win_ref[s:s + n_rot, :]

    first = CONV_HALO - CONV_WIDTH // 2
    lanes = 128
    for r in range(tm // rs):
        parts = []
        for c in range(D_CONV // lanes):
            cs = slice(c * lanes, (c + 1) * lanes)
            acc = jnp.zeros((rs, lanes), F32)
            for t in range(CONV_WIDTH):
                off = r * rs + first + t
                base = off - off % 8
                acc = acc + rot_ref[off % 8, base:base + rs, cs] * w_ref[t:t + 1, cs]
            parts.append(acc)
        y = jnp.concatenate(parts, axis=-1) + b_ref[...]
        y = _layer_norm(y, lg_ref[...], lb_ref[...])
        o_ref[r * rs:(r + 1) * rs, :] = (y * jax.nn.sigmoid(y)).astype(BF16)


def _conv_module(zin, w_dw, b_dw, ln_g, ln_b, tm, rs=32):
    n = zin.shape[0]
    nt = n // tm
    hb = tm // CONV_HALO
    n_hb = n // CONV_HALO
    prev_map = lambda col: (lambda i: (jnp.maximum(i * hb - 1, 0), col))
    next_map = lambda col: (lambda i: (jnp.minimum((i + 1) * hb, n_hb - 1), col))
    cur_map = lambda col: (lambda i: (i, col))
    vec = lambda v: v.reshape(1, D_CONV).astype(F32)
    return pl.pallas_call(
        functools.partial(_conv_kernel, tm=tm, rs=rs),
        grid=(nt,),
        in_specs=[
            pl.BlockSpec((CONV_HALO, D_CONV), prev_map(COL_A)),
            pl.BlockSpec((tm, D_CONV), cur_map(COL_A)),
            pl.BlockSpec((CONV_HALO, D_CONV), next_map(COL_A)),
            pl.BlockSpec((CONV_HALO, D_CONV), prev_map(COL_G)),
            pl.BlockSpec((tm, D_CONV), cur_map(COL_G)),
            pl.BlockSpec((CONV_HALO, D_CONV), next_map(COL_G)),
            pl.BlockSpec((CONV_WIDTH, D_CONV), lambda i: (0, 0)),
            pl.BlockSpec((1, D_CONV), lambda i: (0, 0)),
            pl.BlockSpec((1, D_CONV), lambda i: (0, 0)),
            pl.BlockSpec((1, D_CONV), lambda i: (0, 0)),
        ],
        out_specs=pl.BlockSpec((tm, D_CONV), lambda i: (i, 0)),
        out_shape=jax.ShapeDtypeStruct((n, D_CONV), BF16),
        scratch_shapes=[pltpu.VMEM((tm + 2 * CONV_HALO, D_CONV), F32),
                        pltpu.VMEM((8, tm + 2 * CONV_HALO - 8, D_CONV), F32)],
        compiler_params=_cparams(("parallel",)),
        name="conv_module",
    )(zin, zin, zin, zin, zin, zin, w_dw.astype(F32), vec(b_dw), vec(ln_g), vec(ln_b))


def _gate_mix_kernel(*refs, n_jobs):
    (h_ref, yna_ref, yf_ref, yc_ref, wg0_ref, wg1_ref, wg2_ref, bg0_ref, bg1_ref, bg2_ref,
     wna_ref, wf_ref, wc_ref) = refs[:13]
    src_refs = refs[13:13 + n_jobs]
    m_ref = refs[13 + n_jobs]
    dst_refs = refs[14 + n_jobs:14 + 2 * n_jobs]
    _cast_jobs(src_refs, dst_refs)
    h = h_ref[...]
    g0 = jax.nn.sigmoid(_dot(h, wg0_ref[0]) + bg0_ref[...])
    m = g0 * _dot(yna_ref[...], wna_ref[0])
    g1 = jax.nn.sigmoid(_dot(h, wg1_ref[0]) + bg1_ref[...])
    m = m + g1 * _dot(yf_ref[...].astype(BF16), wf_ref[0])
    g2 = jax.nn.sigmoid(_dot(h, wg2_ref[0]) + bg2_ref[...])
    m = m + g2 * _dot(yc_ref[...], wc_ref[0])
    m_ref[...] = m.astype(BF16)


def _gate_mix(h, y_na, y_f, y_c, w_gate, b_gate, w_b_na, w_b_f, w_b_c, tm, jobs=()):
    n, d = h.shape
    nc, _, cw = w_b_na.shape
    n_rt = n // tm
    ga = lambda br: (lambda c, i: (br * nc + c, 0, 0))
    gb = lambda br: (lambda c, i: (0, br * nc + c))
    pa = lambda c, i: (c, 0, 0)
    row = lambda c, i: (i, 0)
    bg = b_gate.reshape(1, -1)
    job_specs = [_cast_job_specs(src, layer, cb, nc * n_rt, lambda c, i: c * n_rt + i) for src, layer, cb in jobs]
    outs = pl.pallas_call(
        functools.partial(_gate_mix_kernel, n_jobs=len(jobs)),
        grid=(nc, n_rt),
        in_specs=[
            pl.BlockSpec((tm, d), row),
            pl.BlockSpec((tm, D_NA), row),
            pl.BlockSpec((tm, D_FOURIER), row),
            pl.BlockSpec((tm, D_CONV), row),
            pl.BlockSpec((1, d, cw), ga(0)),
            pl.BlockSpec((1, d, cw), ga(1)),
            pl.BlockSpec((1, d, cw), ga(2)),
            pl.BlockSpec((1, cw), gb(0)),
            pl.BlockSpec((1, cw), gb(1)),
            pl.BlockSpec((1, cw), gb(2)),
            pl.BlockSpec((1, D_NA, cw), pa),
            pl.BlockSpec((1, D_FOURIER, cw), pa),
            pl.BlockSpec((1, D_CONV, cw), pa),
        ] + [js[0] for js in job_specs],
        out_specs=[pl.BlockSpec((tm, cw), lambda c, i: (i, c))] + [js[1] for js in job_specs],
        out_shape=[jax.ShapeDtypeStruct((n, d), BF16)] + [js[2] for js in job_specs],
        compiler_params=_cparams(("arbitrary", "arbitrary")),
        name="gate_mix",
    )(h, y_na, y_f, y_c, w_gate, w_gate, w_gate, bg, bg, bg, w_b_na, w_b_f, w_b_c, *[src for src, _, _ in jobs])
    return outs[0], list(outs[1:])


def _out_proj_kernel(m_ref, x_ref, mod_ref, wo_ref, g_ref, b_ref, o_ref):
    nc, _, cw = wo_ref.shape
    m = m_ref[...]
    for c in range(nc):
        cs = slice(c * cw, (c + 1) * cw)
        o_ref[:, cs] = ALPHA * x_ref[:, cs] + mod_ref[5:6, cs] * _dot(m, wo_ref[c])
    _residual_layer_norm(lambda rows, cs: o_ref[rows, cs], g_ref, b_ref, o_ref, cw, min(o_ref.shape[0], 256))


def _out_proj(m, x, mod, w_o, ln_g, ln_b, tm):
    n, d = x.shape
    row = lambda i: (i, 0)
    fixed = lambda i: (0, 0)
    return pl.pallas_call(
        _out_proj_kernel,
        grid=(n // tm,),
        in_specs=[
            pl.BlockSpec((tm, d), row),
            pl.BlockSpec((tm, d), row),
            pl.BlockSpec(mod.shape, fixed),
            pl.BlockSpec(w_o.shape, lambda i: (0, 0, 0)),
            pl.BlockSpec((1, d), fixed),
            pl.BlockSpec((1, d), fixed),
        ],
        out_specs=pl.BlockSpec((tm, d), row),
        out_shape=jax.ShapeDtypeStruct((n, d), F32),
        compiler_params=_cparams(("parallel",)),
        name="out_proj",
    )(m, x, mod, w_o, ln_g.reshape(1, d), ln_b.reshape(1, d))


def _merge(x, h, mod, y_na, y_f, y_c, w_gate, b_gate, w_b_na, w_b_f, w_b_c, w_o, ln_g, ln_b, tm_mix, tm_out, jobs=()):
    m, cast = _gate_mix(h, y_na, y_f, y_c, w_gate, b_gate, w_b_na, w_b_f, w_b_c, tm=tm_mix, jobs=jobs)
    return _out_proj(m, x, mod, w_o, ln_g, ln_b, tm=tm_out), cast


def _row_tile(n, target):
    t = min(n, target)
    assert n % t == 0
    return t


def kernel(x, c, ctx, c_ctx, w_ada, b_ada, ffn1_w1, ffn1_w3, ffn1_w2, ffn2_w1, ffn2_w3, ffn2_w2, w_in, w_dw, b_dw, conv_ln_g, conv_ln_b, rpb, w_gate, b_gate, w_b_na, w_b_f, w_b_c, w_o, ln_g, ln_b):
    b_, n, d = x.shape
    assert b_ == 1 and c.shape[0] == 1
    n_ctx = ctx.shape[1]
    depth = w_ada.shape[0]

    c_rows = jnp.zeros((8, d), F32).at[0].set(c[0]).at[1].set(c_ctx)
    mods = _ada_mods(c_rows, w_ada, b_ada)

    tm = _row_tile(n, 1024)
    tmc = _row_tile(n_ctx, 256)
    nk1, nk2 = D_FF // FFN1_TF, D_FF // FFN2_TF
    ctx_kb = lambda tf: max(1, CTX_FFN_BYTES // (d * tf * 2))
    n_in = D_IN // INPROJ_TN
    nc = d // MERGE_CW

    ffn_jobs = lambda w1, w3, w2, l, nk: [(w1, l, nk), (w3, l, nk), (w2, l, 1)]
    mix_jobs = lambda l: [(w_in, l, n_in), (w_gate, l, 3 * nc), (w_b_na, l, nc), (w_b_f, l, nc), (w_b_c, l, nc),
                          (w_o, l, nc)]
    ffn1_w = [_blocked_bf16(ffn1_w1[0], nk1), _blocked_bf16(ffn1_w3[0], nk1), _blocked_bf16(ffn1_w2[0], 1)]

    xl = x[0]
    xc = ctx[0]
    for l in range(depth):
        last = l == depth - 1
        mod = mods[l, 0].reshape(N_MOD_USED, d)
        modc = mods[l, 1].reshape(N_MOD_USED, d)
        conv_w = (w_dw[l], b_dw[l], conv_ln_g[l], conv_ln_b[l])

        xl, cast = _ffn(xl, mod, 0, *ffn1_w, ln_g[l, 0], ln_b[l, 0], tm=tm,
                        jobs=mix_jobs(l) + ffn_jobs(ffn2_w1, ffn2_w3, ffn2_w2, l, nk2))
        w_in_l, wg_l, wna_l, wf_l, wc_l, wo_l = cast[:6]
        ffn2_w = cast[6:]
        xc, _ = _ffn(xc, modc, 0, *ffn1_w, ln_g[l, 0], ln_b[l, 0], tm=tmc, kb=ctx_kb(FFN1_TF))

        zin, hl = _inproj(xl, mod, w_in_l, tm=tm)
        zc, hc = _inproj(xc, modc, w_in_l, tm=tmc)

        merge_w = (wg_l, b_gate[l], wna_l, wf_l, wc_l, wo_l, ln_g[l, 1], ln_b[l, 1])
        y_na = _na_attention(zin, zc, rpb[l])
        y_f = _fourier_latent(zin, n1=n // GRID_W, n2=GRID_W)
        y_c = _conv_module(zin, *conv_w, tm=_row_tile(n, 512))
        next_jobs = [] if last else ffn_jobs(ffn1_w1, ffn1_w3, ffn1_w2, l + 1, nk1)
        xl_new, next_w13 = _merge(xl, hl, mod, y_na, y_f, y_c, *merge_w, tm_mix=tm, tm_out=_row_tile(n, 512),
                                  jobs=next_jobs[:2])

        if not last:
            yc_na = _ctx_attention(zc)
            yc_f = _fourier_ctx(zc)
            yc_c = _conv_module(zc, *conv_w, tm=n_ctx)
            xc, _ = _merge(xc, hc, modc, yc_na, yc_f, yc_c, *merge_w, tm_mix=tmc, tm_out=tmc)
        xl = xl_new

        xl, next_w2 = _ffn(xl, mod, 1, *ffn2_w, ln_g[l, 2], ln_b[l, 2], tm=tm, jobs=next_jobs[2:])
        if not last:
            ffn1_w = next_w13 + next_w2
            xc, _ = _ffn(xc, modc, 1, *ffn2_w, ln_g[l, 2], ln_b[l, 2], tm=tmc, kb=ctx_kb(FFN2_TF))
    return xl[None]
```

```python
import functools

import numpy as np
import jax
import jax.numpy as jnp
from jax import lax
from jax.experimental import pallas as pl
from jax.experimental.pallas import tpu as pltpu

F32 = jnp.float32
BF16 = jnp.bfloat16

D_MODEL = 2048
DEPTH = 2
GRID_W = 64
N_HEADS = 8
HEAD_DIM = 128
D_NA = N_HEADS * HEAD_DIM
WIN_R = 8
WIN_C = 16
FOURIER_GROUP = 128
D_FOURIER = 512
D_CONV = 512
CONV_WIDTH = 31
D_FF = 5632
D_IN = 3 * D_NA + D_FOURIER + 2 * D_CONV
N_MOD_USED = 6
ALPHA = (2 * DEPTH) ** 0.25
LN_EPS = 1e-5
NEG_INF = -1e30

COL_F = 3 * D_NA // D_FOURIER
COL_A = COL_F + 1
COL_G = COL_F + 2

LOG2E = float(np.log2(np.e))
NA_HEADS_PER_STEP = 4
Q_ROWS = 4
KW_ROWS = 12
CONV_HALO = 16
FFN1_TF = 256
FFN2_TF = 512
CTX_FFN_BYTES = 2 * 1024 * 1024
ADA_RK = 128
INPROJ_TN = 1536
MERGE_CW = 512

VMEM_LIMIT_V7X = 60 * 1024 * 1024


def _cparams(sem):
    return pltpu.CompilerParams(dimension_semantics=sem, vmem_limit_bytes=VMEM_LIMIT_V7X)


def _layer_norm(r, g, b):
    mu = jnp.mean(r, axis=-1, keepdims=True)
    d = r - mu
    var = jnp.mean(d * d, axis=-1, keepdims=True)
    return d * lax.rsqrt(var + LN_EPS) * g + b


def _dot(a, b):
    return jnp.dot(a, b, preferred_element_type=F32)


def _dot_nt(a, b):
    return lax.dot_general(a, b, (((1,), (1,)), ((), ())), preferred_element_type=F32)


def _ada_kernel(c_ref, w_ref, b_ref, o_ref):
    @pl.when(pl.program_id(1) == 0)
    def _():
        o_ref[0] = jnp.broadcast_to(b_ref[0], o_ref.shape[1:])

    c = c_ref[...]
    s = (c * jax.nn.sigmoid(c)).astype(BF16)
    o_ref[0] += _dot(s, w_ref[0].astype(BF16))


def _ada_mods(c_rows, w_ada, b_ada, rk=ADA_RK):
    n_l, d, _ = w_ada.shape
    n_out = N_MOD_USED * d
    return pl.pallas_call(
        _ada_kernel,
        grid=(n_l, d // rk),
        in_specs=[
            pl.BlockSpec((8, rk), lambda l, k: (0, k)),
            pl.BlockSpec((1, rk, n_out), lambda l, k: (l, k, 0)),
            pl.BlockSpec((1, 1, n_out), lambda l, k: (l, 0, 0)),
        ],
        out_specs=pl.BlockSpec((1, 8, n_out), lambda l, k: (l, 0, 0)),
        out_shape=jax.ShapeDtypeStruct((n_l, 8, n_out), F32),
        compiler_params=_cparams(("parallel", "arbitrary")),
        name="ada_mods",
    )(c_rows, w_ada, b_ada.reshape(n_l, 1, -1))


def _residual_layer_norm(pre_norm, g_ref, b_ref, o_ref, cw, rsub):
    tm, d = o_ref.shape
    chunks = [slice(c * cw, (c + 1) * cw) for c in range(d // cw)]

    def body(r, carry):
        rows = pl.ds(pl.multiple_of(r * rsub, rsub), rsub)
        total = 0.0
        for cs in chunks:
            v = pre_norm(rows, cs)
            o_ref[rows, cs] = v
            total = total + jnp.sum(v, axis=-1, keepdims=True)
        mu = total / d
        sq = 0.0
        for cs in chunks:
            dev = o_ref[rows, cs] - mu
            sq = sq + jnp.sum(dev * dev, axis=-1, keepdims=True)
        inv = lax.rsqrt(sq / d + LN_EPS)
        for cs in chunks:
            o_ref[rows, cs] = (o_ref[rows, cs] - mu) * inv * g_ref[:, cs] + b_ref[:, cs]
        return carry

    lax.fori_loop(0, tm // rsub, body, 0)


def _cast_jobs(src_refs, dst_refs):
    for s_ref, d_ref in zip(src_refs, dst_refs):
        w = d_ref.shape[2]
        for c in range(d_ref.shape[0]):
            d_ref[c] = s_ref[0, :, c * w:(c + 1) * w].astype(BF16)


def _cast_job_specs(src, layer, cb, n_steps, step_of):
    _, r, c = src.shape
    rb = next(v for v in range(16, r + 1, 16) if r % v == 0 and r // v <= n_steps)
    nb = r // rb
    blk = lambda *g: jnp.minimum(step_of(*g), nb - 1)
    in_spec = pl.BlockSpec((1, rb, c), lambda *g: (layer, blk(*g), 0))
    out_spec = pl.BlockSpec((cb, rb, c // cb), lambda *g: (0, blk(*g), 0))
    return in_spec, out_spec, jax.ShapeDtypeStruct((cb, r, c // cb), BF16)


def _blocked_bf16(w, cb):
    r, c = w.shape
    return w.astype(BF16).reshape(r, cb, c // cb).transpose(1, 0, 2)


def _ffn_kernel(*refs, j, cw, rsub, n_jobs):
    z_ref, mod_ref, w1_ref, w3_ref, w2_ref, g_ref, b_ref = refs[:7]
    src_refs = refs[7:7 + n_jobs]
    o_ref = refs[7 + n_jobs]
    dst_refs = refs[8 + n_jobs:8 + 2 * n_jobs]
    h_ref = refs[8 + 2 * n_jobs]
    k = pl.program_id(1)
    d = o_ref.shape[1]
    chunks = [slice(c * cw, (c + 1) * cw) for c in range(d // cw)]

    @pl.when(k == 0)
    def _():
        for cs in chunks:
            shift = mod_ref[3 * j:3 * j + 1, cs]
            scale = mod_ref[3 * j + 1:3 * j + 2, cs]
            h_ref[:, cs] = (z_ref[:, cs] * (1.0 + scale) + shift).astype(BF16)
            o_ref[:, cs] = jnp.zeros((o_ref.shape[0], cw), F32)

    _cast_jobs(src_refs, dst_refs)
    h = h_ref[...]
    tf = w1_ref.shape[2]
    for kb in range(w1_ref.shape[0]):
        a = _dot(h, w1_ref[kb])
        b = _dot(h, w3_ref[kb])
        u = (a * jax.nn.sigmoid(a) * b).astype(BF16)

        for cs in chunks:
            o_ref[:, cs] += _dot(u, w2_ref[0, kb * tf:(kb + 1) * tf, cs])

    @pl.when(k == pl.num_programs(1) - 1)
    def _():
        def pre_norm(rows, cs):
            gate = mod_ref[3 * j + 2:3 * j + 3, cs]
            return ALPHA * z_ref[rows, cs] + (0.5 * gate) * o_ref[rows, cs]

        _residual_layer_norm(pre_norm, g_ref, b_ref, o_ref, cw, rsub)


def _ffn(z, mod, j, w1, w3, w2, ln_g, ln_b, tm, jobs=(), kb=1, cw=512):
    n, d = z.shape
    nk, _, tf = w1.shape
    nk //= kb
    step_of = lambda i, k: i * nk + k
    job_specs = [_cast_job_specs(src, layer, cb, (n // tm) * nk, step_of) for src, layer, cb in jobs]
    outs = pl.pallas_call(
        functools.partial(_ffn_kernel, j=j, cw=cw, rsub=min(tm, 256), n_jobs=len(jobs)),
        grid=(n // tm, nk),
        in_specs=[
            pl.BlockSpec((tm, d), lambda i, k: (i, 0)),
            pl.BlockSpec(mod.shape, lambda i, k: (0, 0)),
            pl.BlockSpec((kb, d, tf), lambda i, k: (k, 0, 0)),
            pl.BlockSpec((kb, d, tf), lambda i, k: (k, 0, 0)),
            pl.BlockSpec((1, kb * tf, d), lambda i, k: (0, k, 0)),
            pl.BlockSpec((1, d), lambda i, k: (0, 0)),
            pl.BlockSpec((1, d), lambda i, k: (0, 0)),
        ] + [js[0] for js in job_specs],
        out_specs=[pl.BlockSpec((tm, d), lambda i, k: (i, 0))] + [js[1] for js in job_specs],
        out_shape=[jax.ShapeDtypeStruct((n, d), F32)] + [js[2] for js in job_specs],
        scratch_shapes=[pltpu.VMEM((tm, d), BF16)],
        compiler_params=_cparams(("arbitrary", "arbitrary")),
        name="ffn",
    )(z, mod, w1, w3, w2, ln_g.reshape(1, d), ln_b.reshape(1, d), *[src for src, _, _ in jobs])
    return outs[0], list(outs[1:])


def _inproj_kernel(x_ref, mod_ref, w_ref, o_ref, h_ref, *, cw):
    @pl.when(pl.program_id(1) == 0)
    def _():
        for c in range(x_ref.shape[1] // cw):
            cs = slice(c * cw, (c + 1) * cw)
            h_ref[:, cs] = (x_ref[:, cs] * (1.0 + mod_ref[4:5, cs]) + mod_ref[3:4, cs]).astype(BF16)

    o_ref[...] = _dot(h_ref[...], w_ref[pl.program_id(1)]).astype(BF16)


def _inproj(x, mod, w_in, tm, cw=512):
    n, d = x.shape
    nb, _, tn = w_in.shape
    d_in = nb * tn
    return pl.pallas_call(
        functools.partial(_inproj_kernel, cw=cw),
        grid=(n // tm, nb),
        in_specs=[
            pl.BlockSpec((tm, d), lambda i, c: (i, 0)),
            pl.BlockSpec(mod.shape, lambda i, c: (0, 0)),
            pl.BlockSpec((nb, d, tn), lambda i, c: (0, 0, 0), pipeline_mode=pl.Buffered(1)),
        ],
        out_specs=[pl.BlockSpec((tm, tn), lambda i, c: (i, c)), pl.BlockSpec((tm, d), lambda i, c: (i, 0))],
        out_shape=[jax.ShapeDtypeStruct((n, d_in), BF16), jax.ShapeDtypeStruct((n, d), BF16)],
        compiler_params=_cparams(("parallel", "arbitrary")),
        name="inproj",
    )(x, mod, w_in)


def _na_geometry(rows):
    assert rows % Q_ROWS == 0 and rows >= KW_ROWS and KW_ROWS % 2 == 0
    wr = min(WIN_R, rows)
    n_rb = rows // Q_ROWS
    q0 = np.arange(n_rb) * Q_ROWS
    kstart = np.clip(q0 - WIN_R // 2, 0, rows - KW_ROWS)
    offs = kstart - q0
    uniq = sorted(set(offs.tolist()), reverse=True)
    cfg = np.array([uniq.index(o) for o in offs.tolist()], np.int32)
    dr_cfg, ok_cfg = [], []
    for off in uniq:
        rb = int(np.nonzero(offs == off)[0][0])
        rq = q0[rb] + np.arange(Q_ROWS)
        rk = kstart[rb] + np.arange(KW_ROWS)
        r0 = np.clip(rq - wr // 2, 0, rows - wr)
        ok_cfg.append((rk[None, :] >= r0[:, None]) & (rk[None, :] < r0[:, None] + wr))
        dr_cfg.append(rk[None, :] - rq[:, None] + WIN_R - 1)
    return n_rb, kstart.astype(np.int32), cfg, np.stack(dr_cfg), np.stack(ok_cfg)


def _na_build_bias(h, rpb_ref, t_ref, bias_ref, dr_cfg, ok_cfg):
    n_dr, n_dc = 2 * WIN_R - 1, 2 * WIN_C - 1
    cq = lax.broadcasted_iota(jnp.int32, (GRID_W, 2 * GRID_W), 0)
    lane = lax.broadcasted_iota(jnp.int32, (GRID_W, 2 * GRID_W), 1)
    ck = lane & (GRID_W - 1)
    c0 = jnp.clip(cq - WIN_C // 2, 0, GRID_W - WIN_C)
    col_ok = (ck >= c0) & (ck < c0 + WIN_C)
    dc = jnp.clip(ck - cq, -(WIN_C - 1), WIN_C - 1) + WIN_C - 1
    base = h * (n_dr * n_dc)
    for dr in range(n_dr):
        acc = jnp.zeros((GRID_W, 2 * GRID_W), F32)
        for j in range(n_dc):
            acc = jnp.where(dc == j, rpb_ref[base + dr * n_dc + j], acc)
        t_ref[dr] = jnp.where(col_ok, acc * LOG2E, NEG_INF)
    left = lane < GRID_W
    neg = jnp.full((GRID_W, 2 * GRID_W), NEG_INF, F32)
    for c in range(dr_cfg.shape[0]):
        for rq in range(Q_ROWS):
            for rk in range(0, KW_ROWS, 2):
                even = t_ref[int(dr_cfg[c, rq, rk])] if ok_cfg[c, rq, rk] else neg
                odd = t_ref[int(dr_cfg[c, rq, rk + 1])] if ok_cfg[c, rq, rk + 1] else neg
                bias_ref[c, rq * GRID_W:(rq + 1) * GRID_W, rk * GRID_W:(rk + 2) * GRID_W] = jnp.where(left, even, odd)


def _na_kernel(cfg_ref, ks_ref, rpb_ref, *refs, dr_cfg, ok_cfg, n_jobs):
    q_ref, k_ref, v_ref, kc_ref, vc_ref = refs[:5]
    src_refs = refs[5:5 + n_jobs]
    o_ref = refs[5 + n_jobs]
    dst_refs = refs[6 + n_jobs:6 + 2 * n_jobs]
    t_ref, bias_ref = refs[6 + 2 * n_jobs:]
    hg = pl.program_id(0)
    rb = pl.program_id(1)
    _cast_jobs(src_refs, dst_refs)

    @pl.when(rb == 0)
    def _():
        for hh in range(NA_HEADS_PER_STEP):
            _na_build_bias(hg * NA_HEADS_PER_STEP + hh, rpb_ref, t_ref, bias_ref.at[hh], dr_cfg, ok_cfg)

    k0 = pl.multiple_of(ks_ref[rb] * GRID_W, GRID_W)
    cfg = cfg_ref[rb]
    scale = (HEAD_DIM ** -0.5) * LOG2E
    for hh in range(NA_HEADS_PER_STEP):
        hs = slice(hh * HEAD_DIM, (hh + 1) * HEAD_DIM)
        q = q_ref[:, hs]
        kwin = k_ref[pl.ds(k0, KW_ROWS * GRID_W), hs]
        vwin = v_ref[pl.ds(k0, KW_ROWS * GRID_W), hs]
        s_lat = _dot_nt(q, kwin) * scale + bias_ref[hh, cfg]
        s_ctx = _dot_nt(q, kc_ref[:, hs]) * scale
        m = jnp.maximum(jnp.max(s_lat, axis=-1, keepdims=True), jnp.max(s_ctx, axis=-1, keepdims=True))
        p_lat = jnp.exp2(s_lat - m)
        p_ctx = jnp.exp2(s_ctx - m)
        denom = jnp.sum(p_lat, axis=-1, keepdims=True) + jnp.sum(p_ctx, axis=-1, keepdims=True)
        o = _dot(p_lat.astype(BF16), vwin) + _dot(p_ctx.astype(BF16), vc_ref[:, hs])
        o_ref[:, hs] = (o / denom).astype(BF16)


def _na_attention(zin, zc, rpb_l, jobs=()):
    n = zin.shape[0]
    n_ctx = zc.shape[0]
    rows = n // GRID_W
    n_rb, kstart, cfg, dr_cfg, ok_cfg = _na_geometry(rows)
    tq = Q_ROWS * GRID_W
    hps = NA_HEADS_PER_STEP
    wb = hps * HEAD_DIM
    n_hg = N_HEADS // hps
    job_specs = [_cast_job_specs(src, layer, cb, n_hg * n_rb, lambda g, r, *_: g * n_rb + r)
                 for src, layer, cb in jobs]
    grid_spec = pltpu.PrefetchScalarGridSpec(
        num_scalar_prefetch=3,
        grid=(n_hg, n_rb),
        in_specs=[
            pl.BlockSpec((tq, wb), lambda g, r, *_: (r, g)),
            pl.BlockSpec((n, wb), lambda g, r, *_: (0, n_hg + g)),
            pl.BlockSpec((n, wb), lambda g, r, *_: (0, 2 * n_hg + g)),
            pl.BlockSpec((n_ctx, wb), lambda g, r, *_: (0, n_hg + g)),
            pl.BlockSpec((n_ctx, wb), lambda g, r, *_: (0, 2 * n_hg + g)),
        ] + [js[0] for js in job_specs],
        out_specs=[pl.BlockSpec((tq, wb), lambda g, r, *_: (r, g))] + [js[1] for js in job_specs],
        scratch_shapes=[
            pltpu.VMEM((2 * WIN_R - 1, GRID_W, 2 * GRID_W), F32),
            pltpu.VMEM((hps, dr_cfg.shape[0], tq, KW_ROWS * GRID_W), F32),
        ],
    )
    outs = pl.pallas_call(
        functools.partial(_na_kernel, dr_cfg=dr_cfg, ok_cfg=ok_cfg, n_jobs=len(jobs)),
        grid_spec=grid_spec,
        out_shape=[jax.ShapeDtypeStruct((n, D_NA), BF16)] + [js[2] for js in job_specs],
        compiler_params=_cparams(("arbitrary", "arbitrary")),
        name="na_attention",
    )(jnp.asarray(cfg), jnp.asarray(kstart), rpb_l.astype(F32).reshape(-1), zin, zin, zin, zc, zc,
      *[src for src, _, _ in jobs])
    return outs[0], list(outs[1:])


def _ctx_attn_kernel(q_ref, k_ref, v_ref, o_ref):
    s = _dot_nt(q_ref[...], k_ref[...]) * (HEAD_DIM ** -0.5)
    p = jnp.exp(s - jnp.max(s, axis=-1, keepdims=True))
    o = _dot(p.astype(BF16), v_ref[...])
    o_ref[...] = (o / jnp.sum(p, axis=-1, keepdims=True)).astype(BF16)


def _ctx_attention(zc):
    n_ctx = zc.shape[0]
    hb = D_NA // HEAD_DIM
    return pl.pallas_call(
        _ctx_attn_kernel,
        grid=(N_HEADS,),
        in_specs=[
            pl.BlockSpec((n_ctx, HEAD_DIM), lambda h: (0, h)),
            pl.BlockSpec((n_ctx, HEAD_DIM), lambda h: (0, hb + h)),
            pl.BlockSpec((n_ctx, HEAD_DIM), lambda h: (0, 2 * hb + h)),
        ],
        out_specs=pl.BlockSpec((n_ctx, HEAD_DIM), lambda h: (0, h)),
        out_shape=jax.ShapeDtypeStruct((n_ctx, D_NA), BF16),
        compiler_params=_cparams(("parallel",)),
        name="ctx_attention",
    )(zc, zc, zc)


def _pow2_near(v):
    return 2.0 ** np.round(np.log2(v))


def _bf16_const(a):
    return jnp.asarray(a, F32).astype(BF16)


def _dft_consts(n1, n2):
    k = FOURIER_GROUP
    n_g = D_FOURIER // k
    n = n1 * n2
    sa, sb = 1.0 / _pow2_near(np.sqrt(k)), 1.0 / _pow2_near(np.sqrt(n2))
    jj = np.arange(k)
    ang = 2 * np.pi * ((jj[:, None] * jj[None, :]) % k) / k
    eye = np.eye(n_g)
    a_mat = np.concatenate([np.kron(eye, np.cos(ang)), -np.kron(eye, np.sin(ang))], axis=1) * sa
    ii = np.arange(n2)
    ang2 = 2 * np.pi * ((ii[:, None] * ii[None, :]) % n2) / n2
    c2, s2 = np.cos(ang2), np.sin(ang2)
    b_mat = np.block([[c2, s2], [-s2, c2]]) * sb
    m2 = np.arange(n2)[:, None, None]
    m1 = np.arange(n1)[None, :, None]
    nn1 = np.arange(n1)[None, None, :]
    ang3 = 2 * np.pi * ((nn1 * (n2 * m1 + m2)) % n) / n
    c_mat = np.concatenate([np.cos(ang3), np.sin(ang3)], axis=2)
    out_scale = float(1.0 / (np.sqrt(k * n) * sa * sb))
    return _bf16_const(a_mat), _bf16_const(b_mat), _bf16_const(c_mat), out_scale


def _fa_kernel(f_ref, a_ref, o_ref):
    z = _dot(f_ref[...], a_ref[...])
    o_ref[0] = z[:, :D_FOURIER].astype(BF16)
    o_ref[1] = z[:, D_FOURIER:].astype(BF16)


def _fb_kernel(z_ref, b_ref, o_ref):
    for j in range(z_ref.shape[1]):
        o_ref[:, j, :] = _dot(b_ref[...], z_ref[:, j, :]).astype(BF16)


def _fc_kernel(ur_ref, ui_ref, c_ref, o_ref, *, n1, mb, out_scale):
    for j in range(mb):
        cm = c_ref[j]
        res = _dot(cm[:, :n1], ur_ref[j]) + _dot(cm[:, n1:], ui_ref[j])
        o_ref[:, j, :] = res * out_scale


def _fourier_ctx_kernel(z_ref, c_ref, o_ref, *, n, out_scale):
    o_ref[...] = (_dot(c_ref[:, :n], z_ref[0]) + _dot(c_ref[:, n:], z_ref[1])) * out_scale


def _fourier_stage_a(zin, a_mat, tm):
    n = zin.shape[0]
    return pl.pallas_call(
        _fa_kernel,
        grid=(n // tm,),
        in_specs=[
            pl.BlockSpec((tm, D_FOURIER), lambda i: (i, COL_F)),
            pl.BlockSpec(a_mat.shape, lambda i: (0, 0)),
        ],
        out_specs=pl.BlockSpec((2, tm, D_FOURIER), lambda i: (0, i, 0)),
        out_shape=jax.ShapeDtypeStruct((2, n, D_FOURIER), BF16),
        compiler_params=_cparams(("parallel",)),
        name="fourier_a",
    )(zin, a_mat)


def _fourier_latent(zin, n1, n2, jt=16, mb=8):
    n = zin.shape[0]
    assert n == n1 * n2
    a_mat, b_mat, c_mat, out_scale = _dft_consts(n1, n2)
    zri = _fourier_stage_a(zin, a_mat, tm=min(n, 1024)).reshape(2 * n2, n1, D_FOURIER)
    jt = min(jt, n1)
    u = pl.pallas_call(
        _fb_kernel,
        grid=(n1 // jt,),
        in_specs=[
            pl.BlockSpec((2 * n2, jt, D_FOURIER), lambda c: (0, c, 0)),
            pl.BlockSpec(b_mat.shape, lambda c: (0, 0)),
        ],
        out_specs=pl.BlockSpec((2 * n2, jt, D_FOURIER), lambda c: (0, c, 0)),
        out_shape=jax.ShapeDtypeStruct((2 * n2, n1, D_FOURIER), BF16),
        compiler_params=_cparams(("parallel",)),
        name="fourier_b",
    )(zri, b_mat)
    nb = n2 // mb
    out = pl.pallas_call(
        functools.partial(_fc_kernel, n1=n1, mb=mb, out_scale=out_scale),
        grid=(nb,),
        in_specs=[
            pl.BlockSpec((mb, n1, D_FOURIER), lambda b: (b, 0, 0)),
            pl.BlockSpec((mb, n1, D_FOURIER), lambda b: (nb + b, 0, 0)),
            pl.BlockSpec((mb, n1, 2 * n1), lambda b: (b, 0, 0)),
        ],
        out_specs=pl.BlockSpec((n1, mb, D_FOURIER), lambda b: (0, b, 0)),
        out_shape=jax.ShapeDtypeStruct((n1, n2, D_FOURIER), F32),
        compiler_params=_cparams(("parallel",)),
        name="fourier_c",
    )(u, u, c_mat)
    return out.reshape(n, D_FOURIER)


def _fourier_ctx(zc):
    n = zc.shape[0]
    a_mat, _, _, out_scale = _dft_consts(n, 1)
    ii = np.arange(n)
    ang = 2 * np.pi * ((ii[:, None] * ii[None, :]) % n) / n
    c_mat = _bf16_const(np.concatenate([np.cos(ang), np.sin(ang)], axis=1))
    zri = _fourier_stage_a(zc, a_mat, tm=n)
    return pl.pallas_call(
        functools.partial(_fourier_ctx_kernel, n=n, out_scale=out_scale),
        grid=(1,),
        in_specs=[
            pl.BlockSpec((2, n, D_FOURIER), lambda i: (0, 0, 0)),
            pl.BlockSpec(c_mat.shape, lambda i: (0, 0)),
        ],
        out_specs=pl.BlockSpec((n, D_FOURIER), lambda i: (0, 0)),
        out_shape=jax.ShapeDtypeStruct((n, D_FOURIER), F32),
        compiler_params=_cparams(("arbitrary",)),
        name="fourier_ctx",
    )(zri, c_mat)


def _conv_kernel(*refs, tm, rs, n_jobs):
    ap_ref, ac_ref, an_ref, gp_ref, gc_ref, gn_ref, w_ref, b_ref, lg_ref, lb_ref = refs[:10]
    src_refs = refs[10:10 + n_jobs]
    o_ref = refs[10 + n_jobs]
    dst_refs = refs[11 + n_jobs:11 + 2 * n_jobs]
    win_ref, rot_ref = refs[11 + 2 * n_jobs:]
    i = pl.program_id(0)
    _cast_jobs(src_refs, dst_refs)

    def glu(a_ref, g_ref):
        return a_ref[...].astype(F32) * jax.nn.sigmoid(g_ref[...].astype(F32))

    win_ref[0:CONV_HALO, :] = jnp.where(i == 0, 0.0, glu(ap_ref, gp_ref))
    win_ref[CONV_HALO:CONV_HALO + tm, :] = glu(ac_ref, gc_ref)
    win_ref[CONV_HALO + tm:, :] = jnp.where(i == pl.num_programs(0) - 1, 0.0, glu(an_ref, gn_ref))

    n_rot = rot_ref.shape[1]
    for s in range(8):
        rot_ref[s] = win_ref[s:s + n_rot, :]

    first = CONV_HALO - CONV_WIDTH // 2
    lanes = 128
    for r in range(tm // rs):
        parts = []
        for c in range(D_CONV // lanes):
            cs = slice(c * lanes, (c + 1) * lanes)
            acc = jnp.zeros((rs, lanes), F32)
            for t in range(CONV_WIDTH):
                off = r * rs + first + t
                base = off - off % 8
                acc = acc + rot_ref[off % 8, base:base + rs, cs] * w_ref[t:t + 1, cs]
            parts.append(acc)
        y = jnp.concatenate(parts, axis=-1) + b_ref[...]
        y = _layer_norm(y, lg_ref[...], lb_ref[...])
        o_ref[r * rs:(r + 1) * rs, :] = (y * jax.nn.sigmoid(y)).astype(BF16)


def _conv_module(zin, w_dw, b_dw, ln_g, ln_b, tm, jobs=(), rs=32):
    n = zin.shape[0]
    nt = n // tm
    hb = tm // CONV_HALO
    n_hb = n // CONV_HALO
    prev_map = lambda col: (lambda i: (jnp.maximum(i * hb - 1, 0), col))
    next_map = lambda col: (lambda i: (jnp.minimum((i + 1) * hb, n_hb - 1), col))
    cur_map = lambda col: (lambda i: (i, col))
    vec = lambda v: v.reshape(1, D_CONV).astype(F32)
    job_specs = [_cast_job_specs(src, layer, cb, nt, lambda i: i) for src, layer, cb in jobs]
    outs = pl.pallas_call(
        functools.partial(_conv_kernel, tm=tm, rs=rs, n_jobs=len(jobs)),
        grid=(nt,),
        in_specs=[
            pl.BlockSpec((CONV_HALO, D_CONV), prev_map(COL_A)),
            pl.BlockSpec((tm, D_CONV), cur_map(COL_A)),
            pl.BlockSpec((CONV_HALO, D_CONV), next_map(COL_A)),
            pl.BlockSpec((CONV_HALO, D_CONV), prev_map(COL_G)),
            pl.BlockSpec((tm, D_CONV), cur_map(COL_G)),
            pl.BlockSpec((CONV_HALO, D_CONV), next_map(COL_G)),
            pl.BlockSpec((CONV_WIDTH, D_CONV), lambda i: (0, 0)),
            pl.BlockSpec((1, D_CONV), lambda i: (0, 0)),
            pl.BlockSpec((1, D_CONV), lambda i: (0, 0)),
            pl.BlockSpec((1, D_CONV), lambda i: (0, 0)),
        ] + [js[0] for js in job_specs],
        out_specs=[pl.BlockSpec((tm, D_CONV), lambda i: (i, 0))] + [js[1] for js in job_specs],
        out_shape=[jax.ShapeDtypeStruct((n, D_CONV), BF16)] + [js[2] for js in job_specs],
        scratch_shapes=[pltpu.VMEM((tm + 2 * CONV_HALO, D_CONV), F32),
                        pltpu.VMEM((8, tm + 2 * CONV_HALO - 8, D_CONV), F32)],
        compiler_params=_cparams(("arbitrary",)),
        name="conv_module",
    )(zin, zin, zin, zin, zin, zin, w_dw.astype(F32), vec(b_dw), vec(ln_g), vec(ln_b),
      *[src for src, _, _ in jobs])
    return outs[0], list(outs[1:])


def _gate_mix_kernel(*refs, n_jobs):
    (h_ref, yna_ref, yf_ref, yc_ref, wg0_ref, wg1_ref, wg2_ref, bg0_ref, bg1_ref, bg2_ref,
     wna_ref, wf_ref, wc_ref) = refs[:13]
    src_refs = refs[13:13 + n_jobs]
    m_ref = refs[13 + n_jobs]
    dst_refs = refs[14 + n_jobs:14 + 2 * n_jobs]
    _cast_jobs(src_refs, dst_refs)
    h = h_ref[...]
    g0 = jax.nn.sigmoid(_dot(h, wg0_ref[0]) + bg0_ref[...])
    m = g0 * _dot(yna_ref[...], wna_ref[0])
    g1 = jax.nn.sigmoid(_dot(h, wg1_ref[0]) + bg1_ref[...])
    m = m + g1 * _dot(yf_ref[...].astype(BF16), wf_ref[0])
    g2 = jax.nn.sigmoid(_dot(h, wg2_ref[0]) + bg2_ref[...])
    m = m + g2 * _dot(yc_ref[...], wc_ref[0])
    m_ref[...] = m.astype(BF16)


def _gate_mix(h, y_na, y_f, y_c, w_gate, b_gate, w_b_na, w_b_f, w_b_c, tm, jobs=()):
    n, d = h.shape
    nc, _, cw = w_b_na.shape
    n_rt = n // tm
    ga = lambda br: (lambda c, i: (br * nc + c, 0, 0))
    gb = lambda br: (lambda c, i: (0, br * nc + c))
    pa = lambda c, i: (c, 0, 0)
    row = lambda c, i: (i, 0)
    bg = b_gate.reshape(1, -1)
    job_specs = [_cast_job_specs(src, layer, cb, nc * n_rt, lambda c, i: c * n_rt + i) for src, layer, cb in jobs]
    outs = pl.pallas_call(
        functools.partial(_gate_mix_kernel, n_jobs=len(jobs)),
        grid=(nc, n_rt),
        in_specs=[
            pl.BlockSpec((tm, d), row),
            pl.BlockSpec((tm, D_NA), row),
            pl.BlockSpec((tm, D_FOURIER), row),
            pl.BlockSpec((tm, D_CONV), row),
            pl.BlockSpec((1, d, cw), ga(0)),
            pl.BlockSpec((1, d, cw), ga(1)),
            pl.BlockSpec((1, d, cw), ga(2)),
            pl.BlockSpec((1, cw), gb(0)),
            pl.BlockSpec((1, cw), gb(1)),
            pl.BlockSpec((1, cw), gb(2)),
            pl.BlockSpec((1, D_NA, cw), pa),
            pl.BlockSpec((1, D_FOURIER, cw), pa),
            pl.BlockSpec((1, D_CONV, cw), pa),
        ] + [js[0] for js in job_specs],
        out_specs=[pl.BlockSpec((tm, cw), lambda c, i: (i, c))] + [js[1] for js in job_specs],
        out_shape=[jax.ShapeDtypeStruct((n, d), BF16)] + [js[2] for js in job_specs],
        compiler_params=_cparams(("arbitrary", "arbitrary")),
        name="gate_mix",
    )(h, y_na, y_f, y_c, w_gate, w_gate, w_gate, bg, bg, bg, w_b_na, w_b_f, w_b_c, *[src for src, _, _ in jobs])
    return outs[0], list(outs[1:])


def _out_proj_kernel(m_ref, x_ref, mod_ref, wo_ref, g_ref, b_ref, o_ref):
    nc, _, cw = wo_ref.shape
    m = m_ref[...]
    for c in range(nc):
        cs = slice(c * cw, (c + 1) * cw)
        o_ref[:, cs] = ALPHA * x_ref[:, cs] + mod_ref[5:6, cs] * _dot(m, wo_ref[c])
    _residual_layer_norm(lambda rows, cs: o_ref[rows, cs], g_ref, b_ref, o_ref, cw, min(o_ref.shape[0], 256))


def _out_proj(m, x, mod, w_o, ln_g, ln_b, tm):
    n, d = x.shape
    row = lambda i: (i, 0)
    fixed = lambda i: (0, 0)
    return pl.pallas_call(
        _out_proj_kernel,
        grid=(n // tm,),
        in_specs=[
            pl.BlockSpec((tm, d), row),
            pl.BlockSpec((tm, d), row),
            pl.BlockSpec(mod.shape, fixed),
            pl.BlockSpec(w_o.shape, lambda i: (0, 0, 0)),
            pl.BlockSpec((1, d), fixed),
            pl.BlockSpec((1, d), fixed),
        ],
        out_specs=pl.BlockSpec((tm, d), row),
        out_shape=jax.ShapeDtypeStruct((n, d), F32),
        compiler_params=_cparams(("parallel",)),
        name="out_proj",
    )(m, x, mod, w_o, ln_g.reshape(1, d), ln_b.reshape(1, d))


def _merge(x, h, mod, y_na, y_f, y_c, w_gate, b_gate, w_b_na, w_b_f, w_b_c, w_o, ln_g, ln_b, tm_mix, tm_out, jobs=()):
    m, cast = _gate_mix(h, y_na, y_f, y_c, w_gate, b_gate, w_b_na, w_b_f, w_b_c, tm=tm_mix, jobs=jobs)
    return _out_proj(m, x, mod, w_o, ln_g, ln_b, tm=tm_out), cast


def _row_tile(n, target):
    t = min(n, target)
    assert n % t == 0
    return t


def kernel(x, c, ctx, c_ctx, w_ada, b_ada, ffn1_w1, ffn1_w3, ffn1_w2, ffn2_w1, ffn2_w3, ffn2_w2, w_in, w_dw, b_dw, conv_ln_g, conv_ln_b, rpb, w_gate, b_gate, w_b_na, w_b_f, w_b_c, w_o, ln_g, ln_b):
    b_, n, d = x.shape
    assert b_ == 1 and c.shape[0] == 1
    n_ctx = ctx.shape[1]
    depth = w_ada.shape[0]

    c_rows = jnp.zeros((8, d), F32).at[0].set(c[0]).at[1].set(c_ctx)
    mods = _ada_mods(c_rows, w_ada, b_ada)

    tm = _row_tile(n, 1024)
    tmc = _row_tile(n_ctx, 256)
    nk1, nk2 = D_FF // FFN1_TF, D_FF // FFN2_TF
    ctx_kb = lambda tf: max(1, CTX_FFN_BYTES // (d * tf * 2))
    n_in = D_IN // INPROJ_TN
    nc = d // MERGE_CW

    ffn_jobs = lambda w1, w3, w2, l, nk: [(w1, l, nk), (w3, l, nk), (w2, l, 1)]
    mix_jobs = lambda l: [(w_in, l, n_in), (w_gate, l, 3 * nc), (w_b_na, l, nc), (w_b_f, l, nc), (w_b_c, l, nc),
                          (w_o, l, nc)]
    ffn1_w = [_blocked_bf16(ffn1_w1[0], nk1), _blocked_bf16(ffn1_w3[0], nk1), _blocked_bf16(ffn1_w2[0], 1)]
    ffn1_tf = FFN1_TF
    mix_w = ffn2_w = None

    xl = x[0]
    xc = ctx[0]
    for l in range(depth):
        last = l == depth - 1
        mod = mods[l, 0].reshape(N_MOD_USED, d)
        modc = mods[l, 1].reshape(N_MOD_USED, d)
        conv_w = (w_dw[l], b_dw[l], conv_ln_g[l], conv_ln_b[l])

        jobs = mix_jobs(l) + ffn_jobs(ffn2_w1, ffn2_w3, ffn2_w2, l, nk2) if l == 0 else []
        xl, cast = _ffn(xl, mod, 0, *ffn1_w, ln_g[l, 0], ln_b[l, 0], tm=tm, jobs=jobs)
        if l == 0:
            mix_w, ffn2_w = cast[:6], cast[6:]
        w_in_l, wg_l, wna_l, wf_l, wc_l, wo_l = mix_w
        xc, _ = _ffn(xc, modc, 0, *ffn1_w, ln_g[l, 0], ln_b[l, 0], tm=tmc, kb=ctx_kb(ffn1_tf))

        zin, hl = _inproj(xl, mod, w_in_l, tm=tm)
        zc, hc = _inproj(xc, modc, w_in_l, tm=tmc)

        merge_w = (wg_l, b_gate[l], wna_l, wf_l, wc_l, wo_l, ln_g[l, 1], ln_b[l, 1])
        next_ffn1 = [] if last else ffn_jobs(ffn1_w1, ffn1_w3, ffn1_w2, l + 1, nk2)
        y_na, next_ffn2_w = _na_attention(zin, zc, rpb[l],
                                          jobs=[] if last else ffn_jobs(ffn2_w1, ffn2_w3, ffn2_w2, l + 1, nk2))
        y_f = _fourier_latent(zin, n1=n // GRID_W, n2=GRID_W)
        y_c, next_mix_w = _conv_module(zin, *conv_w, tm=_row_tile(n, 512), jobs=[] if last else mix_jobs(l + 1))
        xl_new, next_w13 = _merge(xl, hl, mod, y_na, y_f, y_c, *merge_w, tm_mix=tm, tm_out=_row_tile(n, 512),
                                  jobs=next_ffn1[:2])

        if not last:
            yc_na = _ctx_attention(zc)
            yc_f = _fourier_ctx(zc)
            yc_c, _ = _conv_module(zc, *conv_w, tm=n_ctx)
            xc, _ = _merge(xc, hc, modc, yc_na, yc_f, yc_c, *merge_w, tm_mix=tmc, tm_out=tmc)
        xl = xl_new

        xl, next_w2 = _ffn(xl, mod, 1, *ffn2_w, ln_g[l, 2], ln_b[l, 2], tm=tm, jobs=next_ffn1[2:])
        if not last:
            xc, _ = _ffn(xc, modc, 1, *ffn2_w, ln_g[l, 2], ln_b[l, 2], tm=tmc, kb=ctx_kb(FFN2_TF))
            ffn1_w, ffn1_tf = next_w13 + next_w2, FFN2_TF
            mix_w, ffn2_w = next_mix_w, next_ffn2_w
    return xl[None]
```

```python
import functools

import numpy as np
import jax
import jax.numpy as jnp
from jax import lax
from jax.experimental import pallas as pl
from jax.experimental.pallas import tpu as pltpu

F32 = jnp.float32
BF16 = jnp.bfloat16

D_MODEL = 2048
DEPTH = 2
GRID_W = 64
N_HEADS = 8
HEAD_DIM = 128
D_NA = N_HEADS * HEAD_DIM
WIN_R = 8
WIN_C = 16
FOURIER_GROUP = 128
D_FOURIER = 512
D_CONV = 512
CONV_WIDTH = 31
D_FF = 5632
D_IN = 3 * D_NA + D_FOURIER + 2 * D_CONV
N_MOD_USED = 6
ALPHA = (2 * DEPTH) ** 0.25
LN_EPS = 1e-5
NEG_INF = -1e30

COL_F = 3 * D_NA // D_FOURIER
COL_A = COL_F + 1
COL_G = COL_F + 2

LOG2E = float(np.log2(np.e))
NA_HEADS_PER_STEP = 4
Q_ROWS = 4
KW_ROWS = 12
CONV_HALO = 16
FFN_TF = 512
CONV_TM = 256
CTX_FFN_BYTES = 2 * 1024 * 1024
ADA_RK = 128
INPROJ_TN = 1536
MERGE_CW = 512

VMEM_LIMIT_V7X = 60 * 1024 * 1024


def _cparams(sem):
    return pltpu.CompilerParams(dimension_semantics=sem, vmem_limit_bytes=VMEM_LIMIT_V7X)


def _layer_norm(r, g, b):
    mu = jnp.mean(r, axis=-1, keepdims=True)
    d = r - mu
    var = jnp.mean(d * d, axis=-1, keepdims=True)
    return d * lax.rsqrt(var + LN_EPS) * g + b


def _dot(a, b):
    return jnp.dot(a, b, preferred_element_type=F32)


def _dot_nt(a, b):
    return lax.dot_general(a, b, (((1,), (1,)), ((), ())), preferred_element_type=F32)


def _ada_kernel(c_ref, w_ref, b_ref, o_ref):
    @pl.when(pl.program_id(1) == 0)
    def _():
        o_ref[0] = jnp.broadcast_to(b_ref[0], o_ref.shape[1:])

    c = c_ref[...]
    s = (c * jax.nn.sigmoid(c)).astype(BF16)
    o_ref[0] += _dot(s, w_ref[0].astype(BF16))


def _ada_mods(c_rows, w_ada, b_ada, rk=ADA_RK):
    n_l, d, _ = w_ada.shape
    n_out = N_MOD_USED * d
    return pl.pallas_call(
        _ada_kernel,
        grid=(n_l, d // rk),
        in_specs=[
            pl.BlockSpec((8, rk), lambda l, k: (0, k)),
            pl.BlockSpec((1, rk, n_out), lambda l, k: (l, k, 0)),
            pl.BlockSpec((1, 1, n_out), lambda l, k: (l, 0, 0)),
        ],
        out_specs=pl.BlockSpec((1, 8, n_out), lambda l, k: (l, 0, 0)),
        out_shape=jax.ShapeDtypeStruct((n_l, 8, n_out), F32),
        compiler_params=_cparams(("parallel", "arbitrary")),
        name="ada_mods",
    )(c_rows, w_ada, b_ada.reshape(n_l, 1, -1))


def _residual_layer_norm(pre_norm, g_ref, b_ref, o_ref, cw, rsub):
    tm, d = o_ref.shape
    chunks = [slice(c * cw, (c + 1) * cw) for c in range(d // cw)]

    def body(r, carry):
        rows = pl.ds(pl.multiple_of(r * rsub, rsub), rsub)
        total = 0.0
        for cs in chunks:
            v = pre_norm(rows, cs)
            o_ref[rows, cs] = v
            total = total + jnp.sum(v, axis=-1, keepdims=True)
        mu = total / d
        sq = 0.0
        for cs in chunks:
            dev = o_ref[rows, cs] - mu
            sq = sq + jnp.sum(dev * dev, axis=-1, keepdims=True)
        inv = lax.rsqrt(sq / d + LN_EPS)
        for cs in chunks:
            o_ref[rows, cs] = (o_ref[rows, cs] - mu) * inv * g_ref[:, cs] + b_ref[:, cs]
        return carry

    lax.fori_loop(0, tm // rsub, body, 0)


def _cast_jobs(src_refs, dst_refs):
    for s_ref, d_ref in zip(src_refs, dst_refs):
        w = d_ref.shape[2]
        for c in range(d_ref.shape[0]):
            d_ref[c] = s_ref[0, :, c * w:(c + 1) * w].astype(BF16)


def _cast_job_specs(src, layer, cb, n_steps, step_of):
    _, r, c = src.shape
    rb = next(v for v in range(16, r + 1, 16) if r % v == 0 and r // v <= n_steps)
    nb = r // rb
    blk = lambda *g: jnp.minimum(step_of(*g), nb - 1)
    in_spec = pl.BlockSpec((1, rb, c), lambda *g: (layer, blk(*g), 0))
    out_spec = pl.BlockSpec((cb, rb, c // cb), lambda *g: (0, blk(*g), 0))
    return in_spec, out_spec, jax.ShapeDtypeStruct((cb, r, c // cb), BF16)


def _blocked_bf16(w, cb):
    r, c = w.shape
    return w.astype(BF16).reshape(r, cb, c // cb).transpose(1, 0, 2)


def _ffn_kernel(*refs, j, cw, rsub, n_jobs):
    z_ref, mod_ref, w1_ref, w3_ref, w2_ref, g_ref, b_ref = refs[:7]
    src_refs = refs[7:7 + n_jobs]
    o_ref = refs[7 + n_jobs]
    dst_refs = refs[8 + n_jobs:8 + 2 * n_jobs]
    h_ref = refs[8 + 2 * n_jobs]
    k = pl.program_id(1)
    d = o_ref.shape[1]
    chunks = [slice(c * cw, (c + 1) * cw) for c in range(d // cw)]

    @pl.when(k == 0)
    def _():
        for cs in chunks:
            shift = mod_ref[3 * j:3 * j + 1, cs]
            scale = mod_ref[3 * j + 1:3 * j + 2, cs]
            h_ref[:, cs] = (z_ref[:, cs] * (1.0 + scale) + shift).astype(BF16)
            o_ref[:, cs] = jnp.zeros((o_ref.shape[0], cw), F32)

    _cast_jobs(src_refs, dst_refs)
    h = h_ref[...]
    tf = w1_ref.shape[2]
    for kb in range(w1_ref.shape[0]):
        a = _dot(h, w1_ref[kb])
        b = _dot(h, w3_ref[kb])
        u = (a * jax.nn.sigmoid(a) * b).astype(BF16)

        for cs in chunks:
            o_ref[:, cs] += _dot(u, w2_ref[0, kb * tf:(kb + 1) * tf, cs])

    @pl.when(k == pl.num_programs(1) - 1)
    def _():
        def pre_norm(rows, cs):
            gate = mod_ref[3 * j + 2:3 * j + 3, cs]
            return ALPHA * z_ref[rows, cs] + (0.5 * gate) * o_ref[rows, cs]

        _residual_layer_norm(pre_norm, g_ref, b_ref, o_ref, cw, rsub)


def _ffn(z, mod, j, w1, w3, w2, ln_g, ln_b, tm, jobs=(), kb=1, z_buffers=2, cw=512):
    n, d = z.shape
    nk, _, tf = w1.shape
    nk //= kb
    step_of = lambda i, k: i * nk + k
    job_specs = [_cast_job_specs(src, layer, cb, (n // tm) * nk, step_of) for src, layer, cb in jobs]
    outs = pl.pallas_call(
        functools.partial(_ffn_kernel, j=j, cw=cw, rsub=min(tm, 256), n_jobs=len(jobs)),
        grid=(n // tm, nk),
        in_specs=[
            pl.BlockSpec((tm, d), lambda i, k: (i, 0), pipeline_mode=pl.Buffered(z_buffers)),
            pl.BlockSpec(mod.shape, lambda i, k: (0, 0)),
            pl.BlockSpec((kb, d, tf), lambda i, k: (k, 0, 0)),
            pl.BlockSpec((kb, d, tf), lambda i, k: (k, 0, 0)),
            pl.BlockSpec((1, kb * tf, d), lambda i, k: (0, k, 0)),
            pl.BlockSpec((1, d), lambda i, k: (0, 0)),
            pl.BlockSpec((1, d), lambda i, k: (0, 0)),
        ] + [js[0] for js in job_specs],
        out_specs=[pl.BlockSpec((tm, d), lambda i, k: (i, 0))] + [js[1] for js in job_specs],
        out_shape=[jax.ShapeDtypeStruct((n, d), F32)] + [js[2] for js in job_specs],
        scratch_shapes=[pltpu.VMEM((tm, d), BF16)],
        compiler_params=_cparams(("arbitrary", "arbitrary")),
        name="ffn",
    )(z, mod, w1, w3, w2, ln_g.reshape(1, d), ln_b.reshape(1, d), *[src for src, _, _ in jobs])
    return outs[0], list(outs[1:])


def _inproj_kernel(x_ref, mod_ref, w_ref, o_ref, h_ref, *, cw):
    @pl.when(pl.program_id(1) == 0)
    def _():
        for c in range(x_ref.shape[1] // cw):
            cs = slice(c * cw, (c + 1) * cw)
            h_ref[:, cs] = (x_ref[:, cs] * (1.0 + mod_ref[4:5, cs]) + mod_ref[3:4, cs]).astype(BF16)

    o_ref[...] = _dot(h_ref[...], w_ref[pl.program_id(1)]).astype(BF16)


def _inproj(x, mod, w_in, tm, cw=512):
    n, d = x.shape
    nb, _, tn = w_in.shape
    d_in = nb * tn
    return pl.pallas_call(
        functools.partial(_inproj_kernel, cw=cw),
        grid=(n // tm, nb),
        in_specs=[
            pl.BlockSpec((tm, d), lambda i, c: (i, 0)),
            pl.BlockSpec(mod.shape, lambda i, c: (0, 0)),
            pl.BlockSpec((nb, d, tn), lambda i, c: (0, 0, 0), pipeline_mode=pl.Buffered(1)),
        ],
        out_specs=[pl.BlockSpec((tm, tn), lambda i, c: (i, c)), pl.BlockSpec((tm, d), lambda i, c: (i, 0))],
        out_shape=[jax.ShapeDtypeStruct((n, d_in), BF16), jax.ShapeDtypeStruct((n, d), BF16)],
        compiler_params=_cparams(("parallel", "arbitrary")),
        name="inproj",
    )(x, mod, w_in)


def _na_geometry(rows):
    assert rows % Q_ROWS == 0 and rows >= KW_ROWS and KW_ROWS % 2 == 0
    wr = min(WIN_R, rows)
    n_rb = rows // Q_ROWS
    q0 = np.arange(n_rb) * Q_ROWS
    kstart = np.clip(q0 - WIN_R // 2, 0, rows - KW_ROWS)
    offs = kstart - q0
    uniq = sorted(set(offs.tolist()), reverse=True)
    cfg = np.array([uniq.index(o) for o in offs.tolist()], np.int32)
    dr_cfg, ok_cfg = [], []
    for off in uniq:
        rb = int(np.nonzero(offs == off)[0][0])
        rq = q0[rb] + np.arange(Q_ROWS)
        rk = kstart[rb] + np.arange(KW_ROWS)
        r0 = np.clip(rq - wr // 2, 0, rows - wr)
        ok_cfg.append((rk[None, :] >= r0[:, None]) & (rk[None, :] < r0[:, None] + wr))
        dr_cfg.append(rk[None, :] - rq[:, None] + WIN_R - 1)
    return n_rb, kstart.astype(np.int32), cfg, np.stack(dr_cfg), np.stack(ok_cfg)


def _na_build_bias(h, rpb_ref, t_ref, bias_ref, dr_cfg, ok_cfg):
    n_dr, n_dc = 2 * WIN_R - 1, 2 * WIN_C - 1
    cq = lax.broadcasted_iota(jnp.int32, (GRID_W, 2 * GRID_W), 0)
    lane = lax.broadcasted_iota(jnp.int32, (GRID_W, 2 * GRID_W), 1)
    ck = lane & (GRID_W - 1)
    c0 = jnp.clip(cq - WIN_C // 2, 0, GRID_W - WIN_C)
    col_ok = (ck >= c0) & (ck < c0 + WIN_C)
    dc = jnp.clip(ck - cq, -(WIN_C - 1), WIN_C - 1) + WIN_C - 1
    base = h * (n_dr * n_dc)
    for dr in range(n_dr):
        acc = jnp.zeros((GRID_W, 2 * GRID_W), F32)
        for j in range(n_dc):
            acc = jnp.where(dc == j, rpb_ref[base + dr * n_dc + j], acc)
        t_ref[dr] = jnp.where(col_ok, acc * LOG2E, NEG_INF)
    left = lane < GRID_W
    neg = jnp.full((GRID_W, 2 * GRID_W), NEG_INF, F32)
    for c in range(dr_cfg.shape[0]):
        for rq in range(Q_ROWS):
            for rk in range(0, KW_ROWS, 2):
                even = t_ref[int(dr_cfg[c, rq, rk])] if ok_cfg[c, rq, rk] else neg
                odd = t_ref[int(dr_cfg[c, rq, rk + 1])] if ok_cfg[c, rq, rk + 1] else neg
                bias_ref[c, rq * GRID_W:(rq + 1) * GRID_W, rk * GRID_W:(rk + 2) * GRID_W] = jnp.where(left, even, odd)


def _na_kernel(cfg_ref, ks_ref, rpb_ref, *refs, dr_cfg, ok_cfg, n_jobs):
    q_ref, k_ref, v_ref, kc_ref, vc_ref = refs[:5]
    src_refs = refs[5:5 + n_jobs]
    o_ref = refs[5 + n_jobs]
    dst_refs = refs[6 + n_jobs:6 + 2 * n_jobs]
    t_ref, bias_ref = refs[6 + 2 * n_jobs:]
    hg = pl.program_id(0)
    rb = pl.program_id(1)
    _cast_jobs(src_refs, dst_refs)

    @pl.when(rb == 0)
    def _():
        for hh in range(NA_HEADS_PER_STEP):
            _na_build_bias(hg * NA_HEADS_PER_STEP + hh, rpb_ref, t_ref, bias_ref.at[hh], dr_cfg, ok_cfg)

    k0 = pl.multiple_of(ks_ref[rb] * GRID_W, GRID_W)
    cfg = cfg_ref[rb]
    scale = (HEAD_DIM ** -0.5) * LOG2E
    for hh in range(NA_HEADS_PER_STEP):
        hs = slice(hh * HEAD_DIM, (hh + 1) * HEAD_DIM)
        q = q_ref[:, hs]
        kwin = k_ref[pl.ds(k0, KW_ROWS * GRID_W), hs]
        vwin = v_ref[pl.ds(k0, KW_ROWS * GRID_W), hs]
        s_lat = _dot_nt(q, kwin) * scale + bias_ref[hh, cfg]
        s_ctx = _dot_nt(q, kc_ref[:, hs]) * scale
        m = jnp.maximum(jnp.max(s_lat, axis=-1, keepdims=True), jnp.max(s_ctx, axis=-1, keepdims=True))
        p_lat = jnp.exp2(s_lat - m)
        p_ctx = jnp.exp2(s_ctx - m)
        denom = jnp.sum(p_lat, axis=-1, keepdims=True) + jnp.sum(p_ctx, axis=-1, keepdims=True)
        o = _dot(p_lat.astype(BF16), vwin) + _dot(p_ctx.astype(BF16), vc_ref[:, hs])
        o_ref[:, hs] = (o / denom).astype(BF16)


def _na_attention(zin, zc, rpb_l, jobs=()):
    n = zin.shape[0]
    n_ctx = zc.shape[0]
    rows = n // GRID_W
    n_rb, kstart, cfg, dr_cfg, ok_cfg = _na_geometry(rows)
    tq = Q_ROWS * GRID_W
    hps = NA_HEADS_PER_STEP
    wb = hps * HEAD_DIM
    n_hg = N_HEADS // hps
    job_specs = [_cast_job_specs(src, layer, cb, n_hg * n_rb, lambda g, r, *_: g * n_rb + r)
                 for src, layer, cb in jobs]
    grid_spec = pltpu.PrefetchScalarGridSpec(
        num_scalar_prefetch=3,
        grid=(n_hg, n_rb),
        in_specs=[
            pl.BlockSpec((tq, wb), lambda g, r, *_: (r, g)),
            pl.BlockSpec((n, wb), lambda g, r, *_: (0, n_hg + g)),
            pl.BlockSpec((n, wb), lambda g, r, *_: (0, 2 * n_hg + g)),
            pl.BlockSpec((n_ctx, wb), lambda g, r, *_: (0, n_hg + g)),
            pl.BlockSpec((n_ctx, wb), lambda g, r, *_: (0, 2 * n_hg + g)),
        ] + [js[0] for js in job_specs],
        out_specs=[pl.BlockSpec((tq, wb), lambda g, r, *_: (r, g))] + [js[1] for js in job_specs],
        scratch_shapes=[
            pltpu.VMEM((2 * WIN_R - 1, GRID_W, 2 * GRID_W), F32),
            pltpu.VMEM((hps, dr_cfg.shape[0], tq, KW_ROWS * GRID_W), F32),
        ],
    )
    outs = pl.pallas_call(
        functools.partial(_na_kernel, dr_cfg=dr_cfg, ok_cfg=ok_cfg, n_jobs=len(jobs)),
        grid_spec=grid_spec,
        out_shape=[jax.ShapeDtypeStruct((n, D_NA), BF16)] + [js[2] for js in job_specs],
        compiler_params=_cparams(("arbitrary", "arbitrary")),
        name="na_attention",
    )(jnp.asarray(cfg), jnp.asarray(kstart), rpb_l.astype(F32).reshape(-1), zin, zin, zin, zc, zc,
      *[src for src, _, _ in jobs])
    return outs[0], list(outs[1:])


def _ctx_attn_kernel(q_ref, k_ref, v_ref, o_ref):
    s = _dot_nt(q_ref[...], k_ref[...]) * (HEAD_DIM ** -0.5)
    p = jnp.exp(s - jnp.max(s, axis=-1, keepdims=True))
    o = _dot(p.astype(BF16), v_ref[...])
    o_ref[...] = (o / jnp.sum(p, axis=-1, keepdims=True)).astype(BF16)


def _ctx_attention(zc):
    n_ctx = zc.shape[0]
    hb = D_NA // HEAD_DIM
    return pl.pallas_call(
        _ctx_attn_kernel,
        grid=(N_HEADS,),
        in_specs=[
            pl.BlockSpec((n_ctx, HEAD_DIM), lambda h: (0, h)),
            pl.BlockSpec((n_ctx, HEAD_DIM), lambda h: (0, hb + h)),
            pl.BlockSpec((n_ctx, HEAD_DIM), lambda h: (0, 2 * hb + h)),
        ],
        out_specs=pl.BlockSpec((n_ctx, HEAD_DIM), lambda h: (0, h)),
        out_shape=jax.ShapeDtypeStruct((n_ctx, D_NA), BF16),
        compiler_params=_cparams(("parallel",)),
        name="ctx_attention",
    )(zc, zc, zc)


def _pow2_near(v):
    return 2.0 ** np.round(np.log2(v))


def _bf16_const(a):
    return jnp.asarray(a, F32).astype(BF16)


def _dft_consts(n1, n2):
    k = FOURIER_GROUP
    n_g = D_FOURIER // k
    n = n1 * n2
    sa, sb = 1.0 / _pow2_near(np.sqrt(k)), 1.0 / _pow2_near(np.sqrt(n2))
    jj = np.arange(k)
    ang = 2 * np.pi * ((jj[:, None] * jj[None, :]) % k) / k
    eye = np.eye(n_g)
    a_mat = np.concatenate([np.kron(eye, np.cos(ang)), -np.kron(eye, np.sin(ang))], axis=1) * sa
    ii = np.arange(n2)
    ang2 = 2 * np.pi * ((ii[:, None] * ii[None, :]) % n2) / n2
    c2, s2 = np.cos(ang2), np.sin(ang2)
    b_mat = np.block([[c2, s2], [-s2, c2]]) * sb
    m2 = np.arange(n2)[:, None, None]
    m1 = np.arange(n1)[None, :, None]
    nn1 = np.arange(n1)[None, None, :]
    ang3 = 2 * np.pi * ((nn1 * (n2 * m1 + m2)) % n) / n
    c_mat = np.concatenate([np.cos(ang3), np.sin(ang3)], axis=2)
    out_scale = float(1.0 / (np.sqrt(k * n) * sa * sb))
    return _bf16_const(a_mat), _bf16_const(b_mat), _bf16_const(c_mat), out_scale


def _fa_kernel(f_ref, a_ref, o_ref):
    z = _dot(f_ref[...], a_ref[...])
    o_ref[0] = z[:, :D_FOURIER].astype(BF16)
    o_ref[1] = z[:, D_FOURIER:].astype(BF16)


def _fb_kernel(z_ref, b_ref, o_ref):
    for j in range(z_ref.shape[1]):
        o_ref[:, j, :] = _dot(b_ref[...], z_ref[:, j, :]).astype(BF16)


def _fc_kernel(ur_ref, ui_ref, c_ref, o_ref, *, n1, mb, out_scale):
    for j in range(mb):
        cm = c_ref[j]
        res = _dot(cm[:, :n1], ur_ref[j]) + _dot(cm[:, n1:], ui_ref[j])
        o_ref[:, j, :] = res * out_scale


def _fourier_ctx_kernel(z_ref, c_ref, o_ref, *, n, out_scale):
    o_ref[...] = (_dot(c_ref[:, :n], z_ref[0]) + _dot(c_ref[:, n:], z_ref[1])) * out_scale


def _fourier_stage_a(zin, a_mat, tm):
    n = zin.shape[0]
    return pl.pallas_call(
        _fa_kernel,
        grid=(n // tm,),
        in_specs=[
            pl.BlockSpec((tm, D_FOURIER), lambda i: (i, COL_F)),
            pl.BlockSpec(a_mat.shape, lambda i: (0, 0)),
        ],
        out_specs=pl.BlockSpec((2, tm, D_FOURIER), lambda i: (0, i, 0)),
        out_shape=jax.ShapeDtypeStruct((2, n, D_FOURIER), BF16),
        compiler_params=_cparams(("parallel",)),
        name="fourier_a",
    )(zin, a_mat)


def _fourier_latent(zin, n1, n2, jt=16, mb=8):
    n = zin.shape[0]
    assert n == n1 * n2
    a_mat, b_mat, c_mat, out_scale = _dft_consts(n1, n2)
    zri = _fourier_stage_a(zin, a_mat, tm=min(n, 1024)).reshape(2 * n2, n1, D_FOURIER)
    jt = min(jt, n1)
    u = pl.pallas_call(
        _fb_kernel,
        grid=(n1 // jt,),
        in_specs=[
            pl.BlockSpec((2 * n2, jt, D_FOURIER), lambda c: (0, c, 0)),
            pl.BlockSpec(b_mat.shape, lambda c: (0, 0)),
        ],
        out_specs=pl.BlockSpec((2 * n2, jt, D_FOURIER), lambda c: (0, c, 0)),
        out_shape=jax.ShapeDtypeStruct((2 * n2, n1, D_FOURIER), BF16),
        compiler_params=_cparams(("parallel",)),
        name="fourier_b",
    )(zri, b_mat)
    nb = n2 // mb
    out = pl.pallas_call(
        functools.partial(_fc_kernel, n1=n1, mb=mb, out_scale=out_scale),
        grid=(nb,),
        in_specs=[
            pl.BlockSpec((mb, n1, D_FOURIER), lambda b: (b, 0, 0)),
            pl.BlockSpec((mb, n1, D_FOURIER), lambda b: (nb + b, 0, 0)),
            pl.BlockSpec((mb, n1, 2 * n1), lambda b: (b, 0, 0)),
        ],
        out_specs=pl.BlockSpec((n1, mb, D_FOURIER), lambda b: (0, b, 0)),
        out_shape=jax.ShapeDtypeStruct((n1, n2, D_FOURIER), F32),
        compiler_params=_cparams(("parallel",)),
        name="fourier_c",
    )(u, u, c_mat)
    return out.reshape(n, D_FOURIER)


def _fourier_ctx(zc):
    n = zc.shape[0]
    a_mat, _, _, out_scale = _dft_consts(n, 1)
    ii = np.arange(n)
    ang = 2 * np.pi * ((ii[:, None] * ii[None, :]) % n) / n
    c_mat = _bf16_const(np.concatenate([np.cos(ang), np.sin(ang)], axis=1))
    zri = _fourier_stage_a(zc, a_mat, tm=n)
    return pl.pallas_call(
        functools.partial(_fourier_ctx_kernel, n=n, out_scale=out_scale),
        grid=(1,),
        in_specs=[
            pl.BlockSpec((2, n, D_FOURIER), lambda i: (0, 0, 0)),
            pl.BlockSpec(c_mat.shape, lambda i: (0, 0)),
        ],
        out_specs=pl.BlockSpec((n, D_FOURIER), lambda i: (0, 0)),
        out_shape=jax.ShapeDtypeStruct((n, D_FOURIER), F32),
        compiler_params=_cparams(("arbitrary",)),
        name="fourier_ctx",
    )(zri, c_mat)


def _conv_kernel(*refs, tm, rs, n_jobs):
    ap_ref, ac_ref, an_ref, gp_ref, gc_ref, gn_ref, w_ref, b_ref, lg_ref, lb_ref = refs[:10]
    src_refs = refs[10:10 + n_jobs]
    o_ref = refs[10 + n_jobs]
    dst_refs = refs[11 + n_jobs:11 + 2 * n_jobs]
    win_ref, rot_ref = refs[11 + 2 * n_jobs:]
    i = pl.program_id(0)
    _cast_jobs(src_refs, dst_refs)

    def glu(a_ref, g_ref):
        return a_ref[...].astype(F32) * jax.nn.sigmoid(g_ref[...].astype(F32))

    win_ref[0:CONV_HALO, :] = jnp.where(i == 0, 0.0, glu(ap_ref, gp_ref))
    win_ref[CONV_HALO:CONV_HALO + tm, :] = glu(ac_ref, gc_ref)
    win_ref[CONV_HALO + tm:, :] = jnp.where(i == pl.num_programs(0) - 1, 0.0, glu(an_ref, gn_ref))

    n_rot = rot_ref.shape[1]
    for s in range(8):
        rot_ref[s] = win_ref[s:s + n_rot, :]

    first = CONV_HALO - CONV_WIDTH // 2
    lanes = 128
    for r in range(tm // rs):
        parts = []
        for c in range(D_CONV // lanes):
            cs = slice(c * lanes, (c + 1) * lanes)
            acc = jnp.zeros((rs, lanes), F32)
            for t in range(CONV_WIDTH):
                off = r * rs + first + t
                base = off - off % 8
                acc = acc + rot_ref[off % 8, base:base + rs, cs] * w_ref[t:t + 1, cs]
            parts.append(acc)
        y = jnp.concatenate(parts, axis=-1) + b_ref[...]
        y = _layer_norm(y, lg_ref[...], lb_ref[...])
        o_ref[r * rs:(r + 1) * rs, :] = (y * jax.nn.sigmoid(y)).astype(BF16)


def _conv_module(zin, w_dw, b_dw, ln_g, ln_b, tm, jobs=(), rs=32):
    n = zin.shape[0]
    nt = n // tm
    hb = tm // CONV_HALO
    n_hb = n // CONV_HALO
    prev_map = lambda col: (lambda i: (jnp.maximum(i * hb - 1, 0), col))
    next_map = lambda col: (lambda i: (jnp.minimum((i + 1) * hb, n_hb - 1), col))
    cur_map = lambda col: (lambda i: (i, col))
    vec = lambda v: v.reshape(1, D_CONV).astype(F32)
    job_specs = [_cast_job_specs(src, layer, cb, nt, lambda i: i) for src, layer, cb in jobs]
    outs = pl.pallas_call(
        functools.partial(_conv_kernel, tm=tm, rs=rs, n_jobs=len(jobs)),
        grid=(nt,),
        in_specs=[
            pl.BlockSpec((CONV_HALO, D_CONV), prev_map(COL_A)),
            pl.BlockSpec((tm, D_CONV), cur_map(COL_A)),
            pl.BlockSpec((CONV_HALO, D_CONV), next_map(COL_A)),
            pl.BlockSpec((CONV_HALO, D_CONV), prev_map(COL_G)),
            pl.BlockSpec((tm, D_CONV), cur_map(COL_G)),
            pl.BlockSpec((CONV_HALO, D_CONV), next_map(COL_G)),
            pl.BlockSpec((CONV_WIDTH, D_CONV), lambda i: (0, 0)),
            pl.BlockSpec((1, D_CONV), lambda i: (0, 0)),
            pl.BlockSpec((1, D_CONV), lambda i: (0, 0)),
            pl.BlockSpec((1, D_CONV), lambda i: (0, 0)),
        ] + [js[0] for js in job_specs],
        out_specs=[pl.BlockSpec((tm, D_CONV), lambda i: (i, 0))] + [js[1] for js in job_specs],
        out_shape=[jax.ShapeDtypeStruct((n, D_CONV), BF16)] + [js[2] for js in job_specs],
        scratch_shapes=[pltpu.VMEM((tm + 2 * CONV_HALO, D_CONV), F32),
                        pltpu.VMEM((8, tm + 2 * CONV_HALO - 8, D_CONV), F32)],
        compiler_params=_cparams(("arbitrary",)),
        name="conv_module",
    )(zin, zin, zin, zin, zin, zin, w_dw.astype(F32), vec(b_dw), vec(ln_g), vec(ln_b),
      *[src for src, _, _ in jobs])
    return outs[0], list(outs[1:])


def _gate_mix_kernel(*refs, n_jobs):
    (h_ref, yna_ref, yf_ref, yc_ref, wg0_ref, wg1_ref, wg2_ref, bg0_ref, bg1_ref, bg2_ref,
     wna_ref, wf_ref, wc_ref) = refs[:13]
    src_refs = refs[13:13 + n_jobs]
    m_ref = refs[13 + n_jobs]
    dst_refs = refs[14 + n_jobs:14 + 2 * n_jobs]
    _cast_jobs(src_refs, dst_refs)
    h = h_ref[...]
    g0 = jax.nn.sigmoid(_dot(h, wg0_ref[0]) + bg0_ref[...])
    m = g0 * _dot(yna_ref[...], wna_ref[0])
    g1 = jax.nn.sigmoid(_dot(h, wg1_ref[0]) + bg1_ref[...])
    m = m + g1 * _dot(yf_ref[...].astype(BF16), wf_ref[0])
    g2 = jax.nn.sigmoid(_dot(h, wg2_ref[0]) + bg2_ref[...])
    m = m + g2 * _dot(yc_ref[...], wc_ref[0])
    m_ref[...] = m.astype(BF16)


def _gate_mix(h, y_na, y_f, y_c, w_gate, b_gate, w_b_na, w_b_f, w_b_c, tm, jobs=()):
    n, d = h.shape
    nc, _, cw = w_b_na.shape
    n_rt = n // tm
    ga = lambda br: (lambda c, i: (br * nc + c, 0, 0))
    gb = lambda br: (lambda c, i: (0, br * nc + c))
    pa = lambda c, i: (c, 0, 0)
    row = lambda c, i: (i, 0)
    bg = b_gate.reshape(1, -1)
    job_specs = [_cast_job_specs(src, layer, cb, nc * n_rt, lambda c, i: c * n_rt + i) for src, layer, cb in jobs]
    outs = pl.pallas_call(
        functools.partial(_gate_mix_kernel, n_jobs=len(jobs)),
        grid=(nc, n_rt),
        in_specs=[
            pl.BlockSpec((tm, d), row),
            pl.BlockSpec((tm, D_NA), row),
            pl.BlockSpec((tm, D_FOURIER), row),
            pl.BlockSpec((tm, D_CONV), row),
            pl.BlockSpec((1, d, cw), ga(0)),
            pl.BlockSpec((1, d, cw), ga(1)),
            pl.BlockSpec((1, d, cw), ga(2)),
            pl.BlockSpec((1, cw), gb(0)),
            pl.BlockSpec((1, cw), gb(1)),
            pl.BlockSpec((1, cw), gb(2)),
            pl.BlockSpec((1, D_NA, cw), pa),
            pl.BlockSpec((1, D_FOURIER, cw), pa),
            pl.BlockSpec((1, D_CONV, cw), pa),
        ] + [js[0] for js in job_specs],
        out_specs=[pl.BlockSpec((tm, cw), lambda c, i: (i, c))] + [js[1] for js in job_specs],
        out_shape=[jax.ShapeDtypeStruct((n, d), BF16)] + [js[2] for js in job_specs],
        compiler_params=_cparams(("arbitrary", "arbitrary")),
        name="gate_mix",
    )(h, y_na, y_f, y_c, w_gate, w_gate, w_gate, bg, bg, bg, w_b_na, w_b_f, w_b_c, *[src for src, _, _ in jobs])
    return outs[0], list(outs[1:])


def _out_proj_kernel(m_ref, x_ref, mod_ref, wo_ref, g_ref, b_ref, o_ref):
    nc, _, cw = wo_ref.shape
    m = m_ref[...]
    for c in range(nc):
        cs = slice(c * cw, (c + 1) * cw)
        o_ref[:, cs] = ALPHA * x_ref[:, cs] + mod_ref[5:6, cs] * _dot(m, wo_ref[c])
    _residual_layer_norm(lambda rows, cs: o_ref[rows, cs], g_ref, b_ref, o_ref, cw, min(o_ref.shape[0], 256))


def _out_proj(m, x, mod, w_o, ln_g, ln_b, tm):
    n, d = x.shape
    row = lambda i: (i, 0)
    fixed = lambda i: (0, 0)
    return pl.pallas_call(
        _out_proj_kernel,
        grid=(n // tm,),
        in_specs=[
            pl.BlockSpec((tm, d), row),
            pl.BlockSpec((tm, d), row),
            pl.BlockSpec(mod.shape, fixed),
            pl.BlockSpec(w_o.shape, lambda i: (0, 0, 0)),
            pl.BlockSpec((1, d), fixed),
            pl.BlockSpec((1, d), fixed),
        ],
        out_specs=pl.BlockSpec((tm, d), row),
        out_shape=jax.ShapeDtypeStruct((n, d), F32),
        compiler_params=_cparams(("parallel",)),
        name="out_proj",
    )(m, x, mod, w_o, ln_g.reshape(1, d), ln_b.reshape(1, d))


def _merge(x, h, mod, y_na, y_f, y_c, w_gate, b_gate, w_b_na, w_b_f, w_b_c, w_o, ln_g, ln_b, tm_mix, tm_out, jobs=()):
    m, cast = _gate_mix(h, y_na, y_f, y_c, w_gate, b_gate, w_b_na, w_b_f, w_b_c, tm=tm_mix, jobs=jobs)
    return _out_proj(m, x, mod, w_o, ln_g, ln_b, tm=tm_out), cast


def _row_tile(n, target):
    t = min(n, target)
    assert n % t == 0
    return t


def kernel(x, c, ctx, c_ctx, w_ada, b_ada, ffn1_w1, ffn1_w3, ffn1_w2, ffn2_w1, ffn2_w3, ffn2_w2, w_in, w_dw, b_dw, conv_ln_g, conv_ln_b, rpb, w_gate, b_gate, w_b_na, w_b_f, w_b_c, w_o, ln_g, ln_b):
    b_, n, d = x.shape
    assert b_ == 1 and c.shape[0] == 1
    n_ctx = ctx.shape[1]
    depth = w_ada.shape[0]

    c_rows = jnp.zeros((8, d), F32).at[0].set(c[0]).at[1].set(c_ctx)
    mods = _ada_mods(c_rows, w_ada, b_ada)

    tm = _row_tile(n, 1024)
    tmc = _row_tile(n_ctx, 256)
    nk = D_FF // FFN_TF
    ctx_kb = max(1, CTX_FFN_BYTES // (d * FFN_TF * 2))
    n_in = D_IN // INPROJ_TN
    nc = d // MERGE_CW

    ffn_jobs = lambda w1, w3, w2, l: [(w1, l, nk), (w3, l, nk), (w2, l, 1)]
    mix_jobs = lambda l: [(w_in, l, n_in), (w_gate, l, 3 * nc), (w_b_na, l, nc), (w_b_f, l, nc), (w_b_c, l, nc),
                          (w_o, l, nc)]
    ffn1_w = [_blocked_bf16(ffn1_w1[0], nk), _blocked_bf16(ffn1_w3[0], nk), _blocked_bf16(ffn1_w2[0], 1)]
    mix_w = ffn2_w = None

    xl = x[0]
    xc = ctx[0]
    for l in range(depth):
        last = l == depth - 1
        mod = mods[l, 0].reshape(N_MOD_USED, d)
        modc = mods[l, 1].reshape(N_MOD_USED, d)
        conv_w = (w_dw[l], b_dw[l], conv_ln_g[l], conv_ln_b[l])

        if l == 0:
            xl, mix_w = _ffn(xl, mod, 0, *ffn1_w, ln_g[l, 0], ln_b[l, 0], tm=tm, jobs=mix_jobs(l), z_buffers=1)
        else:
            xl, _ = _ffn(xl, mod, 0, *ffn1_w, ln_g[l, 0], ln_b[l, 0], tm=tm)
        w_in_l, wg_l, wna_l, wf_l, wc_l, wo_l = mix_w
        xc, _ = _ffn(xc, modc, 0, *ffn1_w, ln_g[l, 0], ln_b[l, 0], tm=tmc, kb=ctx_kb)

        zin, hl = _inproj(xl, mod, w_in_l, tm=tm)
        zc, hc = _inproj(xc, modc, w_in_l, tm=tmc)

        merge_w = (wg_l, b_gate[l], wna_l, wf_l, wc_l, wo_l, ln_g[l, 1], ln_b[l, 1])
        next_ffn1 = [] if last else ffn_jobs(ffn1_w1, ffn1_w3, ffn1_w2, l + 1)
        y_na, next_ffn2_w = _na_attention(zin, zc, rpb[l],
                                          jobs=[] if last else ffn_jobs(ffn2_w1, ffn2_w3, ffn2_w2, l + 1))
        y_f = _fourier_latent(zin, n1=n // GRID_W, n2=GRID_W)
        conv_jobs = ([] if last else mix_jobs(l + 1)) + (ffn_jobs(ffn2_w1, ffn2_w3, ffn2_w2, l) if l == 0 else [])
        y_c, cast = _conv_module(zin, *conv_w, tm=_row_tile(n, CONV_TM), jobs=conv_jobs)
        next_mix_w = cast[:0 if last else 6]
        if l == 0:
            ffn2_w = cast[len(next_mix_w):]
        xl_new, next_w13 = _merge(xl, hl, mod, y_na, y_f, y_c, *merge_w, tm_mix=tm, tm_out=_row_tile(n, 512),
                                  jobs=next_ffn1[:2])

        if not last:
            yc_na = _ctx_attention(zc)
            yc_f = _fourier_ctx(zc)
            yc_c, _ = _conv_module(zc, *conv_w, tm=n_ctx)
            xc, _ = _merge(xc, hc, modc, yc_na, yc_f, yc_c, *merge_w, tm_mix=tmc, tm_out=tmc)
        xl = xl_new

        xl, next_w2 = _ffn(xl, mod, 1, *ffn2_w, ln_g[l, 2], ln_b[l, 2], tm=tm, jobs=next_ffn1[2:])
        if not last:
            xc, _ = _ffn(xc, modc, 1, *ffn2_w, ln_g[l, 2], ln_b[l, 2], tm=tmc, kb=ctx_kb)
            ffn1_w = next_w13 + next_w2
            mix_w, ffn2_w = next_mix_w, next_ffn2_w
    return xl[None]
```

```python
import functools

import numpy as np
import jax
import jax.numpy as jnp
from jax import lax
from jax.experimental import pallas as pl
from jax.experimental.pallas import tpu as pltpu

F32 = jnp.float32
BF16 = jnp.bfloat16

D_MODEL = 2048
DEPTH = 2
GRID_W = 64
N_HEADS = 8
HEAD_DIM = 128
D_NA = N_HEADS * HEAD_DIM
WIN_R = 8
WIN_C = 16
FOURIER_GROUP = 128
D_FOURIER = 512
D_CONV = 512
CONV_WIDTH = 31
D_FF = 5632
D_IN = 3 * D_NA + D_FOURIER + 2 * D_CONV
N_MOD_USED = 6
ALPHA = (2 * DEPTH) ** 0.25
LN_EPS = 1e-5
NEG_INF = -1e30

COL_F = 3 * D_NA // D_FOURIER
COL_A = COL_F + 1
COL_G = COL_F + 2

LOG2E = float(np.log2(np.e))
NA_HEADS_PER_STEP = 4
Q_ROWS = 4
KW_ROWS = 12
CONV_HALO = 16
FFN1_TF = 256
FFN2_TF = 512
CTX_FFN_BYTES = 2 * 1024 * 1024
ADA_RK = 128
INPROJ_TN = 1536
MERGE_CW = 512

VMEM_LIMIT_V7X = 60 * 1024 * 1024
LANES_V7X = 128


def _cparams(sem):
    return pltpu.CompilerParams(dimension_semantics=sem, vmem_limit_bytes=VMEM_LIMIT_V7X)


def _layer_norm(r, g, b):
    mu = jnp.mean(r, axis=-1, keepdims=True)
    d = r - mu
    var = jnp.mean(d * d, axis=-1, keepdims=True)
    return d * lax.rsqrt(var + LN_EPS) * g + b


def _dot(a, b):
    return jnp.dot(a, b, preferred_element_type=F32)


def _dot_nt(a, b):
    return lax.dot_general(a, b, (((1,), (1,)), ((), ())), preferred_element_type=F32)


def _ada_kernel(ct_ref, w_ref, b_ref, o_ref):
    k = pl.program_id(1)
    n_cond, rk, lanes = ct_ref.shape
    n_out = o_ref.shape[3]

    @pl.when(k == 0)
    def _():
        o_ref[...] = jnp.zeros(o_ref.shape, F32)

    s = []
    for r in range(n_cond):
        c = ct_ref[r]
        s.append(c * jax.nn.sigmoid(c))
    for j in range(n_out // lanes):
        ls = slice(j * lanes, (j + 1) * lanes)
        acc = [o_ref[0, r, :, ls] for r in range(n_cond)]
        for t in range(rk // 8):
            w = w_ref[0, t * 8:(t + 1) * 8, ls]
            for r in range(n_cond):
                acc[r] = acc[r] + w * s[r][t * 8:(t + 1) * 8, :]
        for r in range(n_cond):
            o_ref[0, r, :, ls] = acc[r]

    @pl.when(k == pl.num_programs(1) - 1)
    def _():
        for r in range(n_cond):
            o_ref[0, r, 0:1, :] = jnp.sum(o_ref[0, r], axis=0, keepdims=True) + b_ref[0]


def _ada_mods(c_vecs, w_ada, b_ada, rk=ADA_RK):
    n_l, d, _ = w_ada.shape
    n_cond = c_vecs.shape[0]
    n_out = N_MOD_USED * d
    ct = jnp.broadcast_to(c_vecs[:, :, None], (n_cond, d, LANES_V7X))
    out = pl.pallas_call(
        _ada_kernel,
        grid=(n_l, d // rk),
        in_specs=[
            pl.BlockSpec((n_cond, rk, LANES_V7X), lambda l, k: (0, k, 0)),
            pl.BlockSpec((1, rk, n_out), lambda l, k: (l, k, 0)),
            pl.BlockSpec((1, 1, n_out), lambda l, k: (l, 0, 0)),
        ],
        out_specs=pl.BlockSpec((1, n_cond, 8, n_out), lambda l, k: (l, 0, 0, 0)),
        out_shape=jax.ShapeDtypeStruct((n_l, n_cond, 8, n_out), F32),
        compiler_params=_cparams(("parallel", "arbitrary")),
        name="ada_mods",
    )(ct, w_ada, b_ada.reshape(n_l, 1, -1))
    return out[:, :, 0, :]


def _residual_layer_norm(pre_norm, g_ref, b_ref, o_ref, cw, rsub):
    tm, d = o_ref.shape
    chunks = [slice(c * cw, (c + 1) * cw) for c in range(d // cw)]

    def body(r, carry):
        rows = pl.ds(pl.multiple_of(r * rsub, rsub), rsub)
        total = 0.0
        for cs in chunks:
            v = pre_norm(rows, cs)
            o_ref[rows, cs] = v
            total = total + jnp.sum(v, axis=-1, keepdims=True)
        mu = total / d
        sq = 0.0
        for cs in chunks:
            dev = o_ref[rows, cs] - mu
            sq = sq + jnp.sum(dev * dev, axis=-1, keepdims=True)
        inv = lax.rsqrt(sq / d + LN_EPS)
        for cs in chunks:
            o_ref[rows, cs] = (o_ref[rows, cs] - mu) * inv * g_ref[:, cs] + b_ref[:, cs]
        return carry

    lax.fori_loop(0, tm // rsub, body, 0)


def _cast_jobs(src_refs, dst_refs):
    for s_ref, d_ref in zip(src_refs, dst_refs):
        w = d_ref.shape[2]
        for c in range(d_ref.shape[0]):
            d_ref[c] = s_ref[0, :, c * w:(c + 1) * w].astype(BF16)


def _cast_job_specs(src, layer, cb, n_steps, step_of):
    _, r, c = src.shape
    rb = next(v for v in range(16, r + 1, 16) if r % v == 0 and r // v <= n_steps)
    nb = r // rb
    blk = lambda *g: jnp.minimum(step_of(*g), nb - 1)
    in_spec = pl.BlockSpec((1, rb, c), lambda *g: (layer, blk(*g), 0))
    out_spec = pl.BlockSpec((cb, rb, c // cb), lambda *g: (0, blk(*g), 0))
    return in_spec, out_spec, jax.ShapeDtypeStruct((cb, r, c // cb), BF16)


def _blocked_bf16(w, cb):
    r, c = w.shape
    return w.astype(BF16).reshape(r, cb, c // cb).transpose(1, 0, 2)


def _ffn_kernel(*refs, j, cw, rsub, n_jobs):
    z_ref, mod_ref, w1_ref, w3_ref, w2_ref, g_ref, b_ref = refs[:7]
    src_refs = refs[7:7 + n_jobs]
    o_ref = refs[7 + n_jobs]
    dst_refs = refs[8 + n_jobs:8 + 2 * n_jobs]
    h_ref = refs[8 + 2 * n_jobs]
    k = pl.program_id(1)
    d = o_ref.shape[1]
    chunks = [slice(c * cw, (c + 1) * cw) for c in range(d // cw)]

    @pl.when(k == 0)
    def _():
        for cs in chunks:
            shift = mod_ref[3 * j:3 * j + 1, cs]
            scale = mod_ref[3 * j + 1:3 * j + 2, cs]
            h_ref[:, cs] = (z_ref[:, cs] * (1.0 + scale) + shift).astype(BF16)
            o_ref[:, cs] = jnp.zeros((o_ref.shape[0], cw), F32)

    _cast_jobs(src_refs, dst_refs)
    h = h_ref[...]
    tf = w1_ref.shape[2]
    for kb in range(w1_ref.shape[0]):
        a = _dot(h, w1_ref[kb])
        b = _dot(h, w3_ref[kb])
        u = (a * jax.nn.sigmoid(a) * b).astype(BF16)

        for cs in chunks:
            o_ref[:, cs] += _dot(u, w2_ref[0, kb * tf:(kb + 1) * tf, cs])

    @pl.when(k == pl.num_programs(1) - 1)
    def _():
        def pre_norm(rows, cs):
            gate = mod_ref[3 * j + 2:3 * j + 3, cs]
            return ALPHA * z_ref[rows, cs] + (0.5 * gate) * o_ref[rows, cs]

        _residual_layer_norm(pre_norm, g_ref, b_ref, o_ref, cw, rsub)


def _ffn(z, mod, j, w1, w3, w2, ln_g, ln_b, tm, jobs=(), kb=1, cw=512):
    n, d = z.shape
    nk, _, tf = w1.shape
    nk //= kb
    step_of = lambda i, k: i * nk + k
    job_specs = [_cast_job_specs(src, layer, cb, (n // tm) * nk, step_of) for src, layer, cb in jobs]
    outs = pl.pallas_call(
        functools.partial(_ffn_kernel, j=j, cw=cw, rsub=min(tm, 256), n_jobs=len(jobs)),
        grid=(n // tm, nk),
        in_specs=[
            pl.BlockSpec((tm, d), lambda i, k: (i, 0)),
            pl.BlockSpec(mod.shape, lambda i, k: (0, 0)),
            pl.BlockSpec((kb, d, tf), lambda i, k: (k, 0, 0)),
            pl.BlockSpec((kb, d, tf), lambda i, k: (k, 0, 0)),
            pl.BlockSpec((1, kb * tf, d), lambda i, k: (0, k, 0)),
            pl.BlockSpec((1, d), lambda i, k: (0, 0)),
            pl.BlockSpec((1, d), lambda i, k: (0, 0)),
        ] + [js[0] for js in job_specs],
        out_specs=[pl.BlockSpec((tm, d), lambda i, k: (i, 0))] + [js[1] for js in job_specs],
        out_shape=[jax.ShapeDtypeStruct((n, d), F32)] + [js[2] for js in job_specs],
        scratch_shapes=[pltpu.VMEM((tm, d), BF16)],
        compiler_params=_cparams(("arbitrary", "arbitrary")),
        name="ffn",
    )(z, mod, w1, w3, w2, ln_g.reshape(1, d), ln_b.reshape(1, d), *[src for src, _, _ in jobs])
    return outs[0], list(outs[1:])


def _inproj_kernel(x_ref, mod_ref, w_ref, o_ref, h_ref, *, cw):
    @pl.when(pl.program_id(1) == 0)
    def _():
        for c in range(x_ref.shape[1] // cw):
            cs = slice(c * cw, (c + 1) * cw)
            h_ref[:, cs] = (x_ref[:, cs] * (1.0 + mod_ref[4:5, cs]) + mod_ref[3:4, cs]).astype(BF16)

    o_ref[...] = _dot(h_ref[...], w_ref[pl.program_id(1)]).astype(BF16)


def _inproj(x, mod, w_in, tm, cw=512):
    n, d = x.shape
    nb, _, tn = w_in.shape
    d_in = nb * tn
    return pl.pallas_call(
        functools.partial(_inproj_kernel, cw=cw),
        grid=(n // tm, nb),
        in_specs=[
            pl.BlockSpec((tm, d), lambda i, c: (i, 0)),
            pl.BlockSpec(mod.shape, lambda i, c: (0, 0)),
            pl.BlockSpec((nb, d, tn), lambda i, c: (0, 0, 0), pipeline_mode=pl.Buffered(1)),
        ],
        out_specs=[pl.BlockSpec((tm, tn), lambda i, c: (i, c)), pl.BlockSpec((tm, d), lambda i, c: (i, 0))],
        out_shape=[jax.ShapeDtypeStruct((n, d_in), BF16), jax.ShapeDtypeStruct((n, d), BF16)],
        compiler_params=_cparams(("parallel", "arbitrary")),
        name="inproj",
    )(x, mod, w_in)


def _na_geometry(rows):
    assert rows % Q_ROWS == 0 and rows >= KW_ROWS and KW_ROWS % 2 == 0
    wr = min(WIN_R, rows)
    n_rb = rows // Q_ROWS
    q0 = np.arange(n_rb) * Q_ROWS
    kstart = np.clip(q0 - WIN_R // 2, 0, rows - KW_ROWS)
    offs = kstart - q0
    uniq = sorted(set(offs.tolist()), reverse=True)
    cfg = np.array([uniq.index(o) for o in offs.tolist()], np.int32)
    dr_cfg, ok_cfg = [], []
    for off in uniq:
        rb = int(np.nonzero(offs == off)[0][0])
        rq = q0[rb] + np.arange(Q_ROWS)
        rk = kstart[rb] + np.arange(KW_ROWS)
        r0 = np.clip(rq - wr // 2, 0, rows - wr)
        ok_cfg.append((rk[None, :] >= r0[:, None]) & (rk[None, :] < r0[:, None] + wr))
        dr_cfg.append(rk[None, :] - rq[:, None] + WIN_R - 1)
    return n_rb, kstart.astype(np.int32), cfg, np.stack(dr_cfg), np.stack(ok_cfg)


def _na_build_bias(h, rpb_ref, t_ref, bias_ref, dr_cfg, ok_cfg):
    n_dr, n_dc = 2 * WIN_R - 1, 2 * WIN_C - 1
    cq = lax.broadcasted_iota(jnp.int32, (GRID_W, 2 * GRID_W), 0)
    lane = lax.broadcasted_iota(jnp.int32, (GRID_W, 2 * GRID_W), 1)
    ck = lane & (GRID_W - 1)
    c0 = jnp.clip(cq - WIN_C // 2, 0, GRID_W - WIN_C)
    col_ok = (ck >= c0) & (ck < c0 + WIN_C)
    dc = jnp.clip(ck - cq, -(WIN_C - 1), WIN_C - 1) + WIN_C - 1
    base = h * (n_dr * n_dc)
    for dr in range(n_dr):
        acc = jnp.zeros((GRID_W, 2 * GRID_W), F32)
        for j in range(n_dc):
            acc = jnp.where(dc == j, rpb_ref[base + dr * n_dc + j], acc)
        t_ref[dr] = jnp.where(col_ok, acc * LOG2E, NEG_INF)
    left = lane < GRID_W
    neg = jnp.full((GRID_W, 2 * GRID_W), NEG_INF, F32)
    for c in range(dr_cfg.shape[0]):
        for rq in range(Q_ROWS):
            for rk in range(0, KW_ROWS, 2):
                even = t_ref[int(dr_cfg[c, rq, rk])] if ok_cfg[c, rq, rk] else neg
                odd = t_ref[int(dr_cfg[c, rq, rk + 1])] if ok_cfg[c, rq, rk + 1] else neg
                bias_ref[c, rq * GRID_W:(rq + 1) * GRID_W, rk * GRID_W:(rk + 2) * GRID_W] = jnp.where(left, even, odd)


def _na_kernel(cfg_ref, ks_ref, rpb_ref, *refs, dr_cfg, ok_cfg, n_jobs):
    q_ref, k_ref, v_ref, kc_ref, vc_ref = refs[:5]
    src_refs = refs[5:5 + n_jobs]
    o_ref = refs[5 + n_jobs]
    dst_refs = refs[6 + n_jobs:6 + 2 * n_jobs]
    t_ref, bias_ref = refs[6 + 2 * n_jobs:]
    hg = pl.program_id(0)
    rb = pl.program_id(1)
    _cast_jobs(src_refs, dst_refs)

    @pl.when(rb == 0)
    def _():
        for hh in range(NA_HEADS_PER_STEP):
            _na_build_bias(hg * NA_HEADS_PER_STEP + hh, rpb_ref, t_ref, bias_ref.at[hh], dr_cfg, ok_cfg)

    k0 = pl.multiple_of(ks_ref[rb] * GRID_W, GRID_W)
    cfg = cfg_ref[rb]
    scale = (HEAD_DIM ** -0.5) * LOG2E
    for hh in range(NA_HEADS_PER_STEP):
        hs = slice(hh * HEAD_DIM, (hh + 1) * HEAD_DIM)
        q = q_ref[:, hs]
        kwin = k_ref[pl.ds(k0, KW_ROWS * GRID_W), hs]
        vwin = v_ref[pl.ds(k0, KW_ROWS * GRID_W), hs]
        s_lat = _dot_nt(q, kwin) * scale + bias_ref[hh, cfg]
        s_ctx = _dot_nt(q, kc_ref[:, hs]) * scale
        m = jnp.maximum(jnp.max(s_lat, axis=-1, keepdims=True), jnp.max(s_ctx, axis=-1, keepdims=True))
        p_lat = jnp.exp2(s_lat - m)
        p_ctx = jnp.exp2(s_ctx - m)
        denom = jnp.sum(p_lat, axis=-1, keepdims=True) + jnp.sum(p_ctx, axis=-1, keepdims=True)
        o = _dot(p_lat.astype(BF16), vwin) + _dot(p_ctx.astype(BF16), vc_ref[:, hs])
        o_ref[:, hs] = (o / denom).astype(BF16)


def _na_attention(zin, zc, rpb_l, jobs=()):
    n = zin.shape[0]
    n_ctx = zc.shape[0]
    rows = n // GRID_W
    n_rb, kstart, cfg, dr_cfg, ok_cfg = _na_geometry(rows)
    tq = Q_ROWS * GRID_W
    hps = NA_HEADS_PER_STEP
    wb = hps * HEAD_DIM
    n_hg = N_HEADS // hps
    job_specs = [_cast_job_specs(src, layer, cb, n_hg * n_rb, lambda g, r, *_: g * n_rb + r)
                 for src, layer, cb in jobs]
    grid_spec = pltpu.PrefetchScalarGridSpec(
        num_scalar_prefetch=3,
        grid=(n_hg, n_rb),
        in_specs=[
            pl.BlockSpec((tq, wb), lambda g, r, *_: (r, g)),
            pl.BlockSpec((n, wb), lambda g, r, *_: (0, n_hg + g)),
            pl.BlockSpec((n, wb), lambda g, r, *_: (0, 2 * n_hg + g)),
            pl.BlockSpec((n_ctx, wb), lambda g, r, *_: (0, n_hg + g)),
            pl.BlockSpec((n_ctx, wb), lambda g, r, *_: (0, 2 * n_hg + g)),
        ] + [js[0] for js in job_specs],
        out_specs=[pl.BlockSpec((tq, wb), lambda g, r, *_: (r, g))] + [js[1] for js in job_specs],
        scratch_shapes=[
            pltpu.VMEM((2 * WIN_R - 1, GRID_W, 2 * GRID_W), F32),
            pltpu.VMEM((hps, dr_cfg.shape[0], tq, KW_ROWS * GRID_W), F32),
        ],
    )
    outs = pl.pallas_call(
        functools.partial(_na_kernel, dr_cfg=dr_cfg, ok_cfg=ok_cfg, n_jobs=len(jobs)),
        grid_spec=grid_spec,
        out_shape=[jax.ShapeDtypeStruct((n, D_NA), BF16)] + [js[2] for js in job_specs],
        compiler_params=_cparams(("arbitrary", "arbitrary")),
        name="na_attention",
    )(jnp.asarray(cfg), jnp.asarray(kstart), rpb_l.astype(F32).reshape(-1), zin, zin, zin, zc, zc,
      *[src for src, _, _ in jobs])
    return outs[0], list(outs[1:])


def _ctx_attn_kernel(q_ref, k_ref, v_ref, o_ref):
    s = _dot_nt(q_ref[...], k_ref[...]) * (HEAD_DIM ** -0.5)
    p = jnp.exp(s - jnp.max(s, axis=-1, keepdims=True))
    o = _dot(p.astype(BF16), v_ref[...])
    o_ref[...] = (o / jnp.sum(p, axis=-1, keepdims=True)).astype(BF16)


def _ctx_attention(zc):
    n_ctx = zc.shape[0]
    hb = D_NA // HEAD_DIM
    return pl.pallas_call(
        _ctx_attn_kernel,
        grid=(N_HEADS,),
        in_specs=[
            pl.BlockSpec((n_ctx, HEAD_DIM), lambda h: (0, h)),
            pl.BlockSpec((n_ctx, HEAD_DIM), lambda h: (0, hb + h)),
            pl.BlockSpec((n_ctx, HEAD_DIM), lambda h: (0, 2 * hb + h)),
        ],
        out_specs=pl.BlockSpec((n_ctx, HEAD_DIM), lambda h: (0, h)),
        out_shape=jax.ShapeDtypeStruct((n_ctx, D_NA), BF16),
        compiler_params=_cparams(("parallel",)),
        name="ctx_attention",
    )(zc, zc, zc)


def _pow2_near(v):
    return 2.0 ** np.round(np.log2(v))


def _bf16_const(a):
    return jnp.asarray(a, F32).astype(BF16)


def _dft_consts(n1, n2):
    k = FOURIER_GROUP
    n_g = D_FOURIER // k
    n = n1 * n2
    sa, sb = 1.0 / _pow2_near(np.sqrt(k)), 1.0 / _pow2_near(np.sqrt(n2))
    jj = np.arange(k)
    ang = 2 * np.pi * ((jj[:, None] * jj[None, :]) % k) / k
    eye = np.eye(n_g)
    a_mat = np.concatenate([np.kron(eye, np.cos(ang)), -np.kron(eye, np.sin(ang))], axis=1) * sa
    ii = np.arange(n2)
    ang2 = 2 * np.pi * ((ii[:, None] * ii[None, :]) % n2) / n2
    c2, s2 = np.cos(ang2), np.sin(ang2)
    b_mat = np.block([[c2, s2], [-s2, c2]]) * sb
    m2 = np.arange(n2)[:, None, None]
    m1 = np.arange(n1)[None, :, None]
    nn1 = np.arange(n1)[None, None, :]
    ang3 = 2 * np.pi * ((nn1 * (n2 * m1 + m2)) % n) / n
    c_mat = np.concatenate([np.cos(ang3), np.sin(ang3)], axis=2)
    out_scale = float(1.0 / (np.sqrt(k * n) * sa * sb))
    return _bf16_const(a_mat), _bf16_const(b_mat), _bf16_const(c_mat), out_scale


def _fa_kernel(f_ref, a_ref, o_ref):
    z = _dot(f_ref[...], a_ref[...])
    o_ref[0] = z[:, :D_FOURIER].astype(BF16)
    o_ref[1] = z[:, D_FOURIER:].astype(BF16)


def _fb_kernel(z_ref, b_ref, o_ref):
    for j in range(z_ref.shape[1]):
        o_ref[:, j, :] = _dot(b_ref[...], z_ref[:, j, :]).astype(BF16)


def _fc_kernel(ur_ref, ui_ref, c_ref, o_ref, *, n1, mb, out_scale):
    for j in range(mb):
        cm = c_ref[j]
        res = _dot(cm[:, :n1], ur_ref[j]) + _dot(cm[:, n1:], ui_ref[j])
        o_ref[:, j, :] = res * out_scale


def _fourier_ctx_kernel(z_ref, c_ref, o_ref, *, n, out_scale):
    o_ref[...] = (_dot(c_ref[:, :n], z_ref[0]) + _dot(c_ref[:, n:], z_ref[1])) * out_scale


def _fourier_stage_a(zin, a_mat, tm):
    n = zin.shape[0]
    return pl.pallas_call(
        _fa_kernel,
        grid=(n // tm,),
        in_specs=[
            pl.BlockSpec((tm, D_FOURIER), lambda i: (i, COL_F)),
            pl.BlockSpec(a_mat.shape, lambda i: (0, 0)),
        ],
        out_specs=pl.BlockSpec((2, tm, D_FOURIER), lambda i: (0, i, 0)),
        out_shape=jax.ShapeDtypeStruct((2, n, D_FOURIER), BF16),
        compiler_params=_cparams(("parallel",)),
        name="fourier_a",
    )(zin, a_mat)


def _fourier_latent(zin, n1, n2, jt=16, mb=8):
    n = zin.shape[0]
    assert n == n1 * n2
    a_mat, b_mat, c_mat, out_scale = _dft_consts(n1, n2)
    zri = _fourier_stage_a(zin, a_mat, tm=min(n, 1024)).reshape(2 * n2, n1, D_FOURIER)
    jt = min(jt, n1)
    u = pl.pallas_call(
        _fb_kernel,
        grid=(n1 // jt,),
        in_specs=[
            pl.BlockSpec((2 * n2, jt, D_FOURIER), lambda c: (0, c, 0)),
            pl.BlockSpec(b_mat.shape, lambda c: (0, 0)),
        ],
        out_specs=pl.BlockSpec((2 * n2, jt, D_FOURIER), lambda c: (0, c, 0)),
        out_shape=jax.ShapeDtypeStruct((2 * n2, n1, D_FOURIER), BF16),
        compiler_params=_cparams(("parallel",)),
        name="fourier_b",
    )(zri, b_mat)
    nb = n2 // mb
    out = pl.pallas_call(
        functools.partial(_fc_kernel, n1=n1, mb=mb, out_scale=out_scale),
        grid=(nb,),
        in_specs=[
            pl.BlockSpec((mb, n1, D_FOURIER), lambda b: (b, 0, 0)),
            pl.BlockSpec((mb, n1, D_FOURIER), lambda b: (nb + b, 0, 0)),
            pl.BlockSpec((mb, n1, 2 * n1), lambda b: (b, 0, 0)),
        ],
        out_specs=pl.BlockSpec((n1, mb, D_FOURIER), lambda b: (0, b, 0)),
        out_shape=jax.ShapeDtypeStruct((n1, n2, D_FOURIER), F32),
        compiler_params=_cparams(("parallel",)),
        name="fourier_c",
    )(u, u, c_mat)
    return out.reshape(n, D_FOURIER)


def _fourier_ctx(zc):
    n = zc.shape[0]
    a_mat, _, _, out_scale = _dft_consts(n, 1)
    ii = np.arange(n)
    ang = 2 * np.pi * ((ii[:, None] * ii[None, :]) % n) / n
    c_mat = _bf16_const(np.concatenate([np.cos(ang), np.sin(ang)], axis=1))
    zri = _fourier_stage_a(zc, a_mat, tm=n)
    return pl.pallas_call(
        functools.partial(_fourier_ctx_kernel, n=n, out_scale=out_scale),
        grid=(1,),
        in_specs=[
            pl.BlockSpec((2, n, D_FOURIER), lambda i: (0, 0, 0)),
            pl.BlockSpec(c_mat.shape, lambda i: (0, 0)),
        ],
        out_specs=pl.BlockSpec((n, D_FOURIER), lambda i: (0, 0)),
        out_shape=jax.ShapeDtypeStruct((n, D_FOURIER), F32),
        compiler_params=_cparams(("arbitrary",)),
        name="fourier_ctx",
    )(zri, c_mat)


def _conv_kernel(*refs, tm, rs, n_jobs):
    ap_ref, ac_ref, an_ref, gp_ref, gc_ref, gn_ref, w_ref, b_ref, lg_ref, lb_ref = refs[:10]
    src_refs = refs[10:10 + n_jobs]
    o_ref = refs[10 + n_jobs]
    dst_refs = refs[11 + n_jobs:11 + 2 * n_jobs]
    win_ref, rot_ref = refs[11 + 2 * n_jobs:]
    i = pl.program_id(0)
    _cast_jobs(src_refs, dst_refs)

    def glu(a_ref, g_ref):
        return a_ref[...].astype(F32) * jax.nn.sigmoid(g_ref[...].astype(F32))

    win_ref[0:CONV_HALO, :] = jnp.where(i == 0, 0.0, glu(ap_ref, gp_ref))
    win_ref[CONV_HALO:CONV_HALO + tm, :] = glu(ac_ref, gc_ref)
    win_ref[CONV_HALO + tm:, :] = jnp.where(i == pl.num_programs(0) - 1, 0.0, glu(an_ref, gn_ref))

    n_rot = rot_ref.shape[1]
    for s in range(8):
        rot_ref[s] = win_ref[s:s + n_rot, :]

    first = CONV_HALO - CONV_WIDTH // 2
    lanes = 128
    for r in range(tm // rs):
        parts = []
        for c in range(D_CONV // lanes):
            cs = slice(c * lanes, (c + 1) * lanes)
            acc = jnp.zeros((rs, lanes), F32)
            for t in range(CONV_WIDTH):
                off = r * rs + first + t
                base = off - off % 8
                acc = acc + rot_ref[off % 8, base:base + rs, cs] * w_ref[t:t + 1, cs]
            parts.append(acc)
        y = jnp.concatenate(parts, axis=-1) + b_ref[...]
        y = _layer_norm(y, lg_ref[...], lb_ref[...])
        o_ref[r * rs:(r + 1) * rs, :] = (y * jax.nn.sigmoid(y)).astype(BF16)


def _conv_module(zin, w_dw, b_dw, ln_g, ln_b, tm, jobs=(), rs=32):
    n = zin.shape[0]
    nt = n // tm
    hb = tm // CONV_HALO
    n_hb = n // CONV_HALO
    prev_map = lambda col: (lambda i: (jnp.maximum(i * hb - 1, 0), col))
    next_map = lambda col: (lambda i: (jnp.minimum((i + 1) * hb, n_hb - 1), col))
    cur_map = lambda col: (lambda i: (i, col))
    vec = lambda v: v.reshape(1, D_CONV).astype(F32)
    job_specs = [_cast_job_specs(src, layer, cb, nt, lambda i: i) for src, layer, cb in jobs]
    outs = pl.pallas_call(
        functools.partial(_conv_kernel, tm=tm, rs=rs, n_jobs=len(jobs)),
        grid=(nt,),
        in_specs=[
            pl.BlockSpec((CONV_HALO, D_CONV), prev_map(COL_A)),
            pl.BlockSpec((tm, D_CONV), cur_map(COL_A)),
            pl.BlockSpec((CONV_HALO, D_CONV), next_map(COL_A)),
            pl.BlockSpec((CONV_HALO, D_CONV), prev_map(COL_G)),
            pl.BlockSpec((tm, D_CONV), cur_map(COL_G)),
            pl.BlockSpec((CONV_HALO, D_CONV), next_map(COL_G)),
            pl.BlockSpec((CONV_WIDTH, D_CONV), lambda i: (0, 0)),
            pl.BlockSpec((1, D_CONV), lambda i: (0, 0)),
            pl.BlockSpec((1, D_CONV), lambda i: (0, 0)),
            pl.BlockSpec((1, D_CONV), lambda i: (0, 0)),
        ] + [js[0] for js in job_specs],
        out_specs=[pl.BlockSpec((tm, D_CONV), lambda i: (i, 0))] + [js[1] for js in job_specs],
        out_shape=[jax.ShapeDtypeStruct((n, D_CONV), BF16)] + [js[2] for js in job_specs],
        scratch_shapes=[pltpu.VMEM((tm + 2 * CONV_HALO, D_CONV), F32),
                        pltpu.VMEM((8, tm + 2 * CONV_HALO - 8, D_CONV), F32)],
        compiler_params=_cparams(("arbitrary",)),
        name="conv_module",
    )(zin, zin, zin, zin, zin, zin, w_dw.astype(F32), vec(b_dw), vec(ln_g), vec(ln_b),
      *[src for src, _, _ in jobs])
    return outs[0], list(outs[1:])


def _gate_mix_kernel(*refs, n_jobs):
    (h_ref, yna_ref, yf_ref, yc_ref, wg0_ref, wg1_ref, wg2_ref, bg0_ref, bg1_ref, bg2_ref,
     wna_ref, wf_ref, wc_ref) = refs[:13]
    src_refs = refs[13:13 + n_jobs]
    m_ref = refs[13 + n_jobs]
    dst_refs = refs[14 + n_jobs:14 + 2 * n_jobs]
    _cast_jobs(src_refs, dst_refs)
    h = h_ref[...]
    g0 = jax.nn.sigmoid(_dot(h, wg0_ref[0]) + bg0_ref[...])
    m = g0 * _dot(yna_ref[...], wna_ref[0])
    g1 = jax.nn.sigmoid(_dot(h, wg1_ref[0]) + bg1_ref[...])
    m = m + g1 * _dot(yf_ref[...].astype(BF16), wf_ref[0])
    g2 = jax.nn.sigmoid(_dot(h, wg2_ref[0]) + bg2_ref[...])
    m = m + g2 * _dot(yc_ref[...], wc_ref[0])
    m_ref[...] = m.astype(BF16)


def _gate_mix(h, y_na, y_f, y_c, w_gate, b_gate, w_b_na, w_b_f, w_b_c, tm, jobs=()):
    n, d = h.shape
    nc, _, cw = w_b_na.shape
    n_rt = n // tm
    ga = lambda br: (lambda c, i: (br * nc + c, 0, 0))
    gb = lambda br: (lambda c, i: (0, br * nc + c))
    pa = lambda c, i: (c, 0, 0)
    row = lambda c, i: (i, 0)
    bg = b_gate.reshape(1, -1)
    job_specs = [_cast_job_specs(src, layer, cb, nc * n_rt, lambda c, i: c * n_rt + i) for src, layer, cb in jobs]
    outs = pl.pallas_call(
        functools.partial(_gate_mix_kernel, n_jobs=len(jobs)),
        grid=(nc, n_rt),
        in_specs=[
            pl.BlockSpec((tm, d), row),
            pl.BlockSpec((tm, D_NA), row),
            pl.BlockSpec((tm, D_FOURIER), row),
            pl.BlockSpec((tm, D_CONV), row),
            pl.BlockSpec((1, d, cw), ga(0)),
            pl.BlockSpec((1, d, cw), ga(1)),
            pl.BlockSpec((1, d, cw), ga(2)),
            pl.BlockSpec((1, cw), gb(0)),
            pl.BlockSpec((1, cw), gb(1)),
            pl.BlockSpec((1, cw), gb(2)),
            pl.BlockSpec((1, D_NA, cw), pa),
            pl.BlockSpec((1, D_FOURIER, cw), pa),
            pl.BlockSpec((1, D_CONV, cw), pa),
        ] + [js[0] for js in job_specs],
        out_specs=[pl.BlockSpec((tm, cw), lambda c, i: (i, c))] + [js[1] for js in job_specs],
        out_shape=[jax.ShapeDtypeStruct((n, d), BF16)] + [js[2] for js in job_specs],
        compiler_params=_cparams(("arbitrary", "arbitrary")),
        name="gate_mix",
    )(h, y_na, y_f, y_c, w_gate, w_gate, w_gate, bg, bg, bg, w_b_na, w_b_f, w_b_c, *[src for src, _, _ in jobs])
    return outs[0], list(outs[1:])


def _out_proj_kernel(m_ref, x_ref, mod_ref, wo_ref, g_ref, b_ref, o_ref):
    nc, _, cw = wo_ref.shape
    m = m_ref[...]
    for c in range(nc):
        cs = slice(c * cw, (c + 1) * cw)
        o_ref[:, cs] = ALPHA * x_ref[:, cs] + mod_ref[5:6, cs] * _dot(m, wo_ref[c])
    _residual_layer_norm(lambda rows, cs: o_ref[rows, cs], g_ref, b_ref, o_ref, cw, min(o_ref.shape[0], 256))


def _out_proj(m, x, mod, w_o, ln_g, ln_b, tm):
    n, d = x.shape
    row = lambda i: (i, 0)
    fixed = lambda i: (0, 0)
    return pl.pallas_call(
        _out_proj_kernel,
        grid=(n // tm,),
        in_specs=[
            pl.BlockSpec((tm, d), row),
            pl.BlockSpec((tm, d), row),
            pl.BlockSpec(mod.shape, fixed),
            pl.BlockSpec(w_o.shape, lambda i: (0, 0, 0)),
            pl.BlockSpec((1, d), fixed),
            pl.BlockSpec((1, d), fixed),
        ],
        out_specs=pl.BlockSpec((tm, d), row),
        out_shape=jax.ShapeDtypeStruct((n, d), F32),
        compiler_params=_cparams(("parallel",)),
        name="out_proj",
    )(m, x, mod, w_o, ln_g.reshape(1, d), ln_b.reshape(1, d))


def _merge(x, h, mod, y_na, y_f, y_c, w_gate, b_gate, w_b_na, w_b_f, w_b_c, w_o, ln_g, ln_b, tm_mix, tm_out, jobs=()):
    m, cast = _gate_mix(h, y_na, y_f, y_c, w_gate, b_gate, w_b_na, w_b_f, w_b_c, tm=tm_mix, jobs=jobs)
    return _out_proj(m, x, mod, w_o, ln_g, ln_b, tm=tm_out), cast


def _row_tile(n, target):
    t = min(n, target)
    assert n % t == 0
    return t


def kernel(x, c, ctx, c_ctx, w_ada, b_ada, ffn1_w1, ffn1_w3, ffn1_w2, ffn2_w1, ffn2_w3, ffn2_w2, w_in, w_dw, b_dw, conv_ln_g, conv_ln_b, rpb, w_gate, b_gate, w_b_na, w_b_f, w_b_c, w_o, ln_g, ln_b):
    b_, n, d = x.shape
    assert b_ == 1 and c.shape[0] == 1
    n_ctx = ctx.shape[1]
    depth = w_ada.shape[0]

    mods = _ada_mods(jnp.stack([c[0], c_ctx]), w_ada, b_ada)

    tm = _row_tile(n, 1024)
    tmc = _row_tile(n_ctx, 256)
    nk1, nk2 = D_FF // FFN1_TF, D_FF // FFN2_TF
    ctx_kb = lambda tf: max(1, CTX_FFN_BYTES // (d * tf * 2))
    n_in = D_IN // INPROJ_TN
    nc = d // MERGE_CW

    ffn_jobs = lambda w1, w3, w2, l, nk: [(w1, l, nk), (w3, l, nk), (w2, l, 1)]
    mix_jobs = lambda l: [(w_in, l, n_in), (w_gate, l, 3 * nc), (w_b_na, l, nc), (w_b_f, l, nc), (w_b_c, l, nc),
                          (w_o, l, nc)]
    ffn1_w = [_blocked_bf16(ffn1_w1[0], nk1), _blocked_bf16(ffn1_w3[0], nk1), _blocked_bf16(ffn1_w2[0], 1)]
    ffn1_tf = FFN1_TF
    mix_w = ffn2_w = None

    xl = x[0]
    xc = ctx[0]
    for l in range(depth):
        last = l == depth - 1
        mod = mods[l, 0].reshape(N_MOD_USED, d)
        modc = mods[l, 1].reshape(N_MOD_USED, d)
        conv_w = (w_dw[l], b_dw[l], conv_ln_g[l], conv_ln_b[l])

        jobs = mix_jobs(l) + ffn_jobs(ffn2_w1, ffn2_w3, ffn2_w2, l, nk2) if l == 0 else []
        xl, cast = _ffn(xl, mod, 0, *ffn1_w, ln_g[l, 0], ln_b[l, 0], tm=tm, jobs=jobs)
        if l == 0:
            mix_w, ffn2_w = cast[:6], cast[6:]
        w_in_l, wg_l, wna_l, wf_l, wc_l, wo_l = mix_w
        xc, _ = _ffn(xc, modc, 0, *ffn1_w, ln_g[l, 0], ln_b[l, 0], tm=tmc, kb=ctx_kb(ffn1_tf))

        zin, hl = _inproj(xl, mod, w_in_l, tm=tm)
        zc, hc = _inproj(xc, modc, w_in_l, tm=tmc)

        merge_w = (wg_l, b_gate[l], wna_l, wf_l, wc_l, wo_l, ln_g[l, 1], ln_b[l, 1])
        next_ffn1 = [] if last else ffn_jobs(ffn1_w1, ffn1_w3, ffn1_w2, l + 1, nk2)
        y_na, next_ffn2_w = _na_attention(zin, zc, rpb[l],
                                          jobs=[] if last else ffn_jobs(ffn2_w1, ffn2_w3, ffn2_w2, l + 1, nk2))
        y_f = _fourier_latent(zin, n1=n // GRID_W, n2=GRID_W)
        y_c, next_mix_w = _conv_module(zin, *conv_w, tm=_row_tile(n, 512), jobs=[] if last else mix_jobs(l + 1))
        xl_new, next_w13 = _merge(xl, hl, mod, y_na, y_f, y_c, *merge_w, tm_mix=tm, tm_out=_row_tile(n, 512),
                                  jobs=next_ffn1[:2])

        if not last:
            yc_na = _ctx_attention(zc)
            yc_f = _fourier_ctx(zc)
            yc_c, _ = _conv_module(zc, *conv_w, tm=n_ctx)
            xc, _ = _merge(xc, hc, modc, yc_na, yc_f, yc_c, *merge_w, tm_mix=tmc, tm_out=tmc)
        xl = xl_new

        xl, next_w2 = _ffn(xl, mod, 1, *ffn2_w, ln_g[l, 2], ln_b[l, 2], tm=tm, jobs=next_ffn1[2:])
        if not last:
            xc, _ = _ffn(xc, modc, 1, *ffn2_w, ln_g[l, 2], ln_b[l, 2], tm=tmc, kb=ctx_kb(FFN2_TF))
            ffn1_w, ffn1_tf = next_w13 + next_w2, FFN2_TF
            mix_w, ffn2_w = next_mix_w, next_ffn2_w
    return xl[None]
```

```python
import functools

import numpy as np
import jax
import jax.numpy as jnp
from jax import lax
from jax.experimental import pallas as pl
from jax.experimental.pallas import tpu as pltpu

F32 = jnp.float32
BF16 = jnp.bfloat16

D_MODEL = 2048
DEPTH = 2
GRID_W = 64
N_HEADS = 8
HEAD_DIM = 128
D_NA = N_HEADS * HEAD_DIM
WIN_R = 8
WIN_C = 16
FOURIER_GROUP = 128
D_FOURIER = 512
D_CONV = 512
CONV_WIDTH = 31
D_FF = 5632
D_IN = 3 * D_NA + D_FOURIER + 2 * D_CONV
N_MOD_USED = 6
ALPHA = (2 * DEPTH) ** 0.25
LN_EPS = 1e-5
NEG_INF = -1e30

COL_F = 3 * D_NA // D_FOURIER
COL_A = COL_F + 1
COL_G = COL_F + 2

LOG2E = float(np.log2(np.e))
NA_HEADS_PER_STEP = 4
Q_ROWS = 4
KW_ROWS = 12
CONV_HALO = 16
FFN1_TF = 256
FFN2_TF = 512
CTX_FFN_BYTES = 2 * 1024 * 1024
ADA_RK = 128
INPROJ_TN = 1536
MERGE_CW = 512

VMEM_LIMIT_V7X = 60 * 1024 * 1024


def _cparams(sem):
    return pltpu.CompilerParams(dimension_semantics=sem, vmem_limit_bytes=VMEM_LIMIT_V7X)


def _layer_norm(r, g, b):
    mu = jnp.mean(r, axis=-1, keepdims=True)
    d = r - mu
    var = jnp.mean(d * d, axis=-1, keepdims=True)
    return d * lax.rsqrt(var + LN_EPS) * g + b


def _dot(a, b):
    return jnp.dot(a, b, preferred_element_type=F32)


def _dot_nt(a, b):
    return lax.dot_general(a, b, (((1,), (1,)), ((), ())), preferred_element_type=F32)


def _ada_kernel(c_ref, w_ref, b_ref, o_ref):
    @pl.when(pl.program_id(1) == 0)
    def _():
        o_ref[0] = jnp.broadcast_to(b_ref[0], o_ref.shape[1:])

    c = c_ref[...]
    s = (c * jax.nn.sigmoid(c)).astype(BF16)
    o_ref[0] += _dot(s, w_ref[0].astype(BF16))


def _ada_mods(c_rows, w_ada, b_ada, rk=ADA_RK):
    n_l, d, _ = w_ada.shape
    n_out = N_MOD_USED * d
    return pl.pallas_call(
        _ada_kernel,
        grid=(n_l, d // rk),
        in_specs=[
            pl.BlockSpec((8, rk), lambda l, k: (0, k)),
            pl.BlockSpec((1, rk, n_out), lambda l, k: (l, k, 0)),
            pl.BlockSpec((1, 1, n_out), lambda l, k: (l, 0, 0)),
        ],
        out_specs=pl.BlockSpec((1, 8, n_out), lambda l, k: (l, 0, 0)),
        out_shape=jax.ShapeDtypeStruct((n_l, 8, n_out), F32),
        compiler_params=_cparams(("parallel", "arbitrary")),
        name="ada_mods",
    )(c_rows, w_ada, b_ada.reshape(n_l, 1, -1))


def _residual_layer_norm(pre_norm, g_ref, b_ref, o_ref, cw, rsub):
    tm, d = o_ref.shape
    chunks = [slice(c * cw, (c + 1) * cw) for c in range(d // cw)]

    def body(r, carry):
        rows = pl.ds(pl.multiple_of(r * rsub, rsub), rsub)
        total = 0.0
        for cs in chunks:
            v = pre_norm(rows, cs)
            o_ref[rows, cs] = v
            total = total + jnp.sum(v, axis=-1, keepdims=True)
        mu = total / d
        sq = 0.0
        for cs in chunks:
            dev = o_ref[rows, cs] - mu
            sq = sq + jnp.sum(dev * dev, axis=-1, keepdims=True)
        inv = lax.rsqrt(sq / d + LN_EPS)
        for cs in chunks:
            o_ref[rows, cs] = (o_ref[rows, cs] - mu) * inv * g_ref[:, cs] + b_ref[:, cs]
        return carry

    lax.fori_loop(0, tm // rsub, body, 0)


def _cast_jobs(src_refs, dst_refs):
    for s_ref, d_ref in zip(src_refs, dst_refs):
        w = d_ref.shape[2]
        for c in range(d_ref.shape[0]):
            d_ref[c] = s_ref[0, :, c * w:(c + 1) * w].astype(BF16)


def _cast_job_specs(src, layer, cb, n_steps, step_of):
    _, r, c = src.shape
    rb = next(v for v in range(16, r + 1, 16) if r % v == 0 and r // v <= n_steps)
    nb = r // rb
    blk = lambda *g: jnp.minimum(step_of(*g), nb - 1)
    in_spec = pl.BlockSpec((1, rb, c), lambda *g: (layer, blk(*g), 0))
    out_spec = pl.BlockSpec((cb, rb, c // cb), lambda *g: (0, blk(*g), 0))
    return in_spec, out_spec, jax.ShapeDtypeStruct((cb, r, c // cb), BF16)


def _blocked_bf16(w, cb):
    r, c = w.shape
    return w.astype(BF16).reshape(r, cb, c // cb).transpose(1, 0, 2)


def _ffn_kernel(*refs, j, cw, rsub, n_jobs):
    z_ref, mod_ref, w1_ref, w3_ref, w2_ref, g_ref, b_ref = refs[:7]
    src_refs = refs[7:7 + n_jobs]
    o_ref = refs[7 + n_jobs]
    dst_refs = refs[8 + n_jobs:8 + 2 * n_jobs]
    h_ref = refs[8 + 2 * n_jobs]
    k = pl.program_id(1)
    d = o_ref.shape[1]
    chunks = [slice(c * cw, (c + 1) * cw) for c in range(d // cw)]

    @pl.when(k == 0)
    def _():
        for cs in chunks:
            shift = mod_ref[3 * j:3 * j + 1, cs]
            scale = mod_ref[3 * j + 1:3 * j + 2, cs]
            h_ref[:, cs] = (z_ref[:, cs] * (1.0 + scale) + shift).astype(BF16)

    _cast_jobs(src_refs, dst_refs)
    tf = w1_ref.shape[2]

    def swiglu_step(first):
        h = h_ref[...]
        for kb in range(w1_ref.shape[0]):
            a = _dot(h, w1_ref[kb])
            b = _dot(h, w3_ref[kb])
            u = (a * jax.nn.sigmoid(a) * b).astype(BF16)

            for cs in chunks:
                y = _dot(u, w2_ref[0, kb * tf:(kb + 1) * tf, cs])
                if first and kb == 0:
                    o_ref[:, cs] = y
                else:
                    o_ref[:, cs] += y

    @pl.when(k == 0)
    def _():
        swiglu_step(True)

    @pl.when(k > 0)
    def _():
        swiglu_step(False)

    @pl.when(k == pl.num_programs(1) - 1)
    def _():
        def pre_norm(rows, cs):
            gate = mod_ref[3 * j + 2:3 * j + 3, cs]
            return ALPHA * z_ref[rows, cs] + (0.5 * gate) * o_ref[rows, cs]

        _residual_layer_norm(pre_norm, g_ref, b_ref, o_ref, cw, rsub)


def _ffn(z, mod, j, w1, w3, w2, ln_g, ln_b, tm, jobs=(), kb=1, cw=512):
    n, d = z.shape
    nk, _, tf = w1.shape
    nk //= kb
    step_of = lambda i, k: i * nk + k
    job_specs = [_cast_job_specs(src, layer, cb, (n // tm) * nk, step_of) for src, layer, cb in jobs]
    outs = pl.pallas_call(
        functools.partial(_ffn_kernel, j=j, cw=cw, rsub=min(tm, 256), n_jobs=len(jobs)),
        grid=(n // tm, nk),
        in_specs=[
            pl.BlockSpec((tm, d), lambda i, k: (i, 0)),
            pl.BlockSpec(mod.shape, lambda i, k: (0, 0)),
            pl.BlockSpec((kb, d, tf), lambda i, k: (k, 0, 0)),
            pl.BlockSpec((kb, d, tf), lambda i, k: (k, 0, 0)),
            pl.BlockSpec((1, kb * tf, d), lambda i, k: (0, k, 0)),
            pl.BlockSpec((1, d), lambda i, k: (0, 0)),
            pl.BlockSpec((1, d), lambda i, k: (0, 0)),
        ] + [js[0] for js in job_specs],
        out_specs=[pl.BlockSpec((tm, d), lambda i, k: (i, 0))] + [js[1] for js in job_specs],
        out_shape=[jax.ShapeDtypeStruct((n, d), F32)] + [js[2] for js in job_specs],
        scratch_shapes=[pltpu.VMEM((tm, d), BF16)],
        compiler_params=_cparams(("arbitrary", "arbitrary")),
        name="ffn",
    )(z, mod, w1, w3, w2, ln_g.reshape(1, d), ln_b.reshape(1, d), *[src for src, _, _ in jobs])
    return outs[0], list(outs[1:])


def _inproj_kernel(x_ref, mod_ref, w_ref, o_ref, h_ref, *, cw):
    @pl.when(pl.program_id(1) == 0)
    def _():
        for c in range(x_ref.shape[1] // cw):
            cs = slice(c * cw, (c + 1) * cw)
            h_ref[:, cs] = (x_ref[:, cs] * (1.0 + mod_ref[4:5, cs]) + mod_ref[3:4, cs]).astype(BF16)

    o_ref[...] = _dot(h_ref[...], w_ref[pl.program_id(1)]).astype(BF16)


def _inproj(x, mod, w_in, tm, cw=512):
    n, d = x.shape
    nb, _, tn = w_in.shape
    d_in = nb * tn
    return pl.pallas_call(
        functools.partial(_inproj_kernel, cw=cw),
        grid=(n // tm, nb),
        in_specs=[
            pl.BlockSpec((tm, d), lambda i, c: (i, 0)),
            pl.BlockSpec(mod.shape, lambda i, c: (0, 0)),
            pl.BlockSpec((nb, d, tn), lambda i, c: (0, 0, 0), pipeline_mode=pl.Buffered(1)),
        ],
        out_specs=[pl.BlockSpec((tm, tn), lambda i, c: (i, c)), pl.BlockSpec((tm, d), lambda i, c: (i, 0))],
        out_shape=[jax.ShapeDtypeStruct((n, d_in), BF16), jax.ShapeDtypeStruct((n, d), BF16)],
        compiler_params=_cparams(("parallel", "arbitrary")),
        name="inproj",
    )(x, mod, w_in)


def _na_geometry(rows):
    assert rows % Q_ROWS == 0 and rows >= KW_ROWS and KW_ROWS % 2 == 0
    wr = min(WIN_R, rows)
    n_rb = rows // Q_ROWS
    q0 = np.arange(n_rb) * Q_ROWS
    kstart = np.clip(q0 - WIN_R // 2, 0, rows - KW_ROWS)
    offs = kstart - q0
    uniq = sorted(set(offs.tolist()), reverse=True)
    cfg = np.array([uniq.index(o) for o in offs.tolist()], np.int32)
    dr_cfg, ok_cfg = [], []
    for off in uniq:
        rb = int(np.nonzero(offs == off)[0][0])
        rq = q0[rb] + np.arange(Q_ROWS)
        rk = kstart[rb] + np.arange(KW_ROWS)
        r0 = np.clip(rq - wr // 2, 0, rows - wr)
        ok_cfg.append((rk[None, :] >= r0[:, None]) & (rk[None, :] < r0[:, None] + wr))
        dr_cfg.append(rk[None, :] - rq[:, None] + WIN_R - 1)
    return n_rb, kstart.astype(np.int32), cfg, np.stack(dr_cfg), np.stack(ok_cfg)


def _na_build_bias(h, rpb_ref, t_ref, bias_ref, dr_cfg, ok_cfg):
    n_dr, n_dc = 2 * WIN_R - 1, 2 * WIN_C - 1
    cq = lax.broadcasted_iota(jnp.int32, (GRID_W, 2 * GRID_W), 0)
    lane = lax.broadcasted_iota(jnp.int32, (GRID_W, 2 * GRID_W), 1)
    ck = lane & (GRID_W - 1)
    c0 = jnp.clip(cq - WIN_C // 2, 0, GRID_W - WIN_C)
    col_ok = (ck >= c0) & (ck < c0 + WIN_C)
    dc = jnp.clip(ck - cq, -(WIN_C - 1), WIN_C - 1) + WIN_C - 1
    base = h * (n_dr * n_dc)
    for dr in range(n_dr):
        acc = jnp.zeros((GRID_W, 2 * GRID_W), F32)
        for j in range(n_dc):
            acc = jnp.where(dc == j, rpb_ref[base + dr * n_dc + j], acc)
        t_ref[dr] = jnp.where(col_ok, acc * LOG2E, NEG_INF)
    left = lane < GRID_W
    neg = jnp.full((GRID_W, 2 * GRID_W), NEG_INF, F32)
    for c in range(dr_cfg.shape[0]):
        for rq in range(Q_ROWS):
            for rk in range(0, KW_ROWS, 2):
                even = t_ref[int(dr_cfg[c, rq, rk])] if ok_cfg[c, rq, rk] else neg
                odd = t_ref[int(dr_cfg[c, rq, rk + 1])] if ok_cfg[c, rq, rk + 1] else neg
                bias_ref[c, rq * GRID_W:(rq + 1) * GRID_W, rk * GRID_W:(rk + 2) * GRID_W] = jnp.where(left, even, odd)


def _na_kernel(cfg_ref, ks_ref, rpb_ref, *refs, dr_cfg, ok_cfg, n_jobs):
    q_ref, k_ref, v_ref, kc_ref, vc_ref = refs[:5]
    src_refs = refs[5:5 + n_jobs]
    o_ref = refs[5 + n_jobs]
    dst_refs = refs[6 + n_jobs:6 + 2 * n_jobs]
    t_ref, bias_ref = refs[6 + 2 * n_jobs:]
    hg = pl.program_id(0)
    rb = pl.program_id(1)
    _cast_jobs(src_refs, dst_refs)

    @pl.when(rb == 0)
    def _():
        for hh in range(NA_HEADS_PER_STEP):
            _na_build_bias(hg * NA_HEADS_PER_STEP + hh, rpb_ref, t_ref, bias_ref.at[hh], dr_cfg, ok_cfg)

    k0 = pl.multiple_of(ks_ref[rb] * GRID_W, GRID_W)
    cfg = cfg_ref[rb]
    scale = (HEAD_DIM ** -0.5) * LOG2E
    for hh in range(NA_HEADS_PER_STEP):
        hs = slice(hh * HEAD_DIM, (hh + 1) * HEAD_DIM)
        q = q_ref[:, hs]
        kwin = k_ref[pl.ds(k0, KW_ROWS * GRID_W), hs]
        vwin = v_ref[pl.ds(k0, KW_ROWS * GRID_W), hs]
        s_lat = _dot_nt(q, kwin) * scale + bias_ref[hh, cfg]
        s_ctx = _dot_nt(q, kc_ref[:, hs]) * scale
        m = jnp.maximum(jnp.max(s_lat, axis=-1, keepdims=True), jnp.max(s_ctx, axis=-1, keepdims=True))
        p_lat = jnp.exp2(s_lat - m)
        p_ctx = jnp.exp2(s_ctx - m)
        denom = jnp.sum(p_lat, axis=-1, keepdims=True) + jnp.sum(p_ctx, axis=-1, keepdims=True)
        o = _dot(p_lat.astype(BF16), vwin) + _dot(p_ctx.astype(BF16), vc_ref[:, hs])
        o_ref[:, hs] = (o / denom).astype(BF16)


def _na_attention(zin, zc, rpb_l, jobs=()):
    n = zin.shape[0]
    n_ctx = zc.shape[0]
    rows = n // GRID_W
    n_rb, kstart, cfg, dr_cfg, ok_cfg = _na_geometry(rows)
    tq = Q_ROWS * GRID_W
    hps = NA_HEADS_PER_STEP
    wb = hps * HEAD_DIM
    n_hg = N_HEADS // hps
    job_specs = [_cast_job_specs(src, layer, cb, n_hg * n_rb, lambda g, r, *_: g * n_rb + r)
                 for src, layer, cb in jobs]
    grid_spec = pltpu.PrefetchScalarGridSpec(
        num_scalar_prefetch=3,
        grid=(n_hg, n_rb),
        in_specs=[
            pl.BlockSpec((tq, wb), lambda g, r, *_: (r, g)),
            pl.BlockSpec((n, wb), lambda g, r, *_: (0, n_hg + g)),
            pl.BlockSpec((n, wb), lambda g, r, *_: (0, 2 * n_hg + g)),
            pl.BlockSpec((n_ctx, wb), lambda g, r, *_: (0, n_hg + g)),
            pl.BlockSpec((n_ctx, wb), lambda g, r, *_: (0, 2 * n_hg + g)),
        ] + [js[0] for js in job_specs],
        out_specs=[pl.BlockSpec((tq, wb), lambda g, r, *_: (r, g))] + [js[1] for js in job_specs],
        scratch_shapes=[
            pltpu.VMEM((2 * WIN_R - 1, GRID_W, 2 * GRID_W), F32),
            pltpu.VMEM((hps, dr_cfg.shape[0], tq, KW_ROWS * GRID_W), F32),
        ],
    )
    outs = pl.pallas_call(
        functools.partial(_na_kernel, dr_cfg=dr_cfg, ok_cfg=ok_cfg, n_jobs=len(jobs)),
        grid_spec=grid_spec,
        out_shape=[jax.ShapeDtypeStruct((n, D_NA), BF16)] + [js[2] for js in job_specs],
        compiler_params=_cparams(("arbitrary", "arbitrary")),
        name="na_attention",
    )(jnp.asarray(cfg), jnp.asarray(kstart), rpb_l.astype(F32).reshape(-1), zin, zin, zin, zc, zc,
      *[src for src, _, _ in jobs])
    return outs[0], list(outs[1:])


def _ctx_attn_kernel(q_ref, k_ref, v_ref, o_ref):
    s = _dot_nt(q_ref[...], k_ref[...]) * (HEAD_DIM ** -0.5)
    p = jnp.exp(s - jnp.max(s, axis=-1, keepdims=True))
    o = _dot(p.astype(BF16), v_ref[...])
    o_ref[...] = (o / jnp.sum(p, axis=-1, keepdims=True)).astype(BF16)


def _ctx_attention(zc):
    n_ctx = zc.shape[0]
    hb = D_NA // HEAD_DIM
    return pl.pallas_call(
        _ctx_attn_kernel,
        grid=(N_HEADS,),
        in_specs=[
            pl.BlockSpec((n_ctx, HEAD_DIM), lambda h: (0, h)),
            pl.BlockSpec((n_ctx, HEAD_DIM), lambda h: (0, hb + h)),
            pl.BlockSpec((n_ctx, HEAD_DIM), lambda h: (0, 2 * hb + h)),
        ],
        out_specs=pl.BlockSpec((n_ctx, HEAD_DIM), lambda h: (0, h)),
        out_shape=jax.ShapeDtypeStruct((n_ctx, D_NA), BF16),
        compiler_params=_cparams(("parallel",)),
        name="ctx_attention",
    )(zc, zc, zc)


def _pow2_near(v):
    return 2.0 ** np.round(np.log2(v))


def _bf16_const(a):
    return jnp.asarray(a, F32).astype(BF16)


def _dft_consts(n1, n2):
    k = FOURIER_GROUP
    n_g = D_FOURIER // k
    n = n1 * n2
    sa, sb = 1.0 / _pow2_near(np.sqrt(k)), 1.0 / _pow2_near(np.sqrt(n2))
    jj = np.arange(k)
    ang = 2 * np.pi * ((jj[:, None] * jj[None, :]) % k) / k
    eye = np.eye(n_g)
    a_mat = np.concatenate([np.kron(eye, np.cos(ang)), -np.kron(eye, np.sin(ang))], axis=1) * sa
    ii = np.arange(n2)
    ang2 = 2 * np.pi * ((ii[:, None] * ii[None, :]) % n2) / n2
    c2, s2 = np.cos(ang2), np.sin(ang2)
    b_mat = np.block([[c2, s2], [-s2, c2]]) * sb
    m2 = np.arange(n2)[:, None, None]
    m1 = np.arange(n1)[None, :, None]
    nn1 = np.arange(n1)[None, None, :]
    ang3 = 2 * np.pi * ((nn1 * (n2 * m1 + m2)) % n) / n
    c_mat = np.concatenate([np.cos(ang3), np.sin(ang3)], axis=2)
    out_scale = float(1.0 / (np.sqrt(k * n) * sa * sb))
    return _bf16_const(a_mat), _bf16_const(b_mat), _bf16_const(c_mat), out_scale


def _fa_kernel(f_ref, a_ref, o_ref):
    z = _dot(f_ref[...], a_ref[...])
    o_ref[0] = z[:, :D_FOURIER].astype(BF16)
    o_ref[1] = z[:, D_FOURIER:].astype(BF16)


def _fb_kernel(z_ref, b_ref, o_ref):
    for j in range(z_ref.shape[1]):
        o_ref[:, j, :] = _dot(b_ref[...], z_ref[:, j, :]).astype(BF16)


def _fc_kernel(ur_ref, ui_ref, c_ref, o_ref, *, n1, mb, out_scale):
    for j in range(mb):
        cm = c_ref[j]
        res = _dot(cm[:, :n1], ur_ref[j]) + _dot(cm[:, n1:], ui_ref[j])
        o_ref[:, j, :] = res * out_scale


def _fourier_ctx_kernel(z_ref, c_ref, o_ref, *, n, out_scale):
    o_ref[...] = (_dot(c_ref[:, :n], z_ref[0]) + _dot(c_ref[:, n:], z_ref[1])) * out_scale


def _fourier_stage_a(zin, a_mat, tm):
    n = zin.shape[0]
    return pl.pallas_call(
        _fa_kernel,
        grid=(n // tm,),
        in_specs=[
            pl.BlockSpec((tm, D_FOURIER), lambda i: (i, COL_F)),
            pl.BlockSpec(a_mat.shape, lambda i: (0, 0)),
        ],
        out_specs=pl.BlockSpec((2, tm, D_FOURIER), lambda i: (0, i, 0)),
        out_shape=jax.ShapeDtypeStruct((2, n, D_FOURIER), BF16),
        compiler_params=_cparams(("parallel",)),
        name="fourier_a",
    )(zin, a_mat)


def _fourier_latent(zin, n1, n2, jt=16, mb=8):
    n = zin.shape[0]
    assert n == n1 * n2
    a_mat, b_mat, c_mat, out_scale = _dft_consts(n1, n2)
    zri = _fourier_stage_a(zin, a_mat, tm=min(n, 1024)).reshape(2 * n2, n1, D_FOURIER)
    jt = min(jt, n1)
    u = pl.pallas_call(
        _fb_kernel,
        grid=(n1 // jt,),
        in_specs=[
            pl.BlockSpec((2 * n2, jt, D_FOURIER), lambda c: (0, c, 0)),
            pl.BlockSpec(b_mat.shape, lambda c: (0, 0)),
        ],
        out_specs=pl.BlockSpec((2 * n2, jt, D_FOURIER), lambda c: (0, c, 0)),
        out_shape=jax.ShapeDtypeStruct((2 * n2, n1, D_FOURIER), BF16),
        compiler_params=_cparams(("parallel",)),
        name="fourier_b",
    )(zri, b_mat)
    nb = n2 // mb
    out = pl.pallas_call(
        functools.partial(_fc_kernel, n1=n1, mb=mb, out_scale=out_scale),
        grid=(nb,),
        in_specs=[
            pl.BlockSpec((mb, n1, D_FOURIER), lambda b: (b, 0, 0)),
            pl.BlockSpec((mb, n1, D_FOURIER), lambda b: (nb + b, 0, 0)),
            pl.BlockSpec((mb, n1, 2 * n1), lambda b: (b, 0, 0)),
        ],
        out_specs=pl.BlockSpec((n1, mb, D_FOURIER), lambda b: (0, b, 0)),
        out_shape=jax.ShapeDtypeStruct((n1, n2, D_FOURIER), F32),
        compiler_params=_cparams(("parallel",)),
        name="fourier_c",
    )(u, u, c_mat)
    return out.reshape(n, D_FOURIER)


def _fourier_ctx(zc):
    n = zc.shape[0]
    a_mat, _, _, out_scale = _dft_consts(n, 1)
    ii = np.arange(n)
    ang = 2 * np.pi * ((ii[:, None] * ii[None, :]) % n) / n
    c_mat = _bf16_const(np.concatenate([np.cos(ang), np.sin(ang)], axis=1))
    zri = _fourier_stage_a(zc, a_mat, tm=n)
    return pl.pallas_call(
        functools.partial(_fourier_ctx_kernel, n=n, out_scale=out_scale),
        grid=(1,),
        in_specs=[
            pl.BlockSpec((2, n, D_FOURIER), lambda i: (0, 0, 0)),
            pl.BlockSpec(c_mat.shape, lambda i: (0, 0)),
        ],
        out_specs=pl.BlockSpec((n, D_FOURIER), lambda i: (0, 0)),
        out_shape=jax.ShapeDtypeStruct((n, D_FOURIER), F32),
        compiler_params=_cparams(("arbitrary",)),
        name="fourier_ctx",
    )(zri, c_mat)


def _conv_kernel(*refs, tm, rs, n_jobs):
    ap_ref, ac_ref, an_ref, gp_ref, gc_ref, gn_ref, w_ref, b_ref, lg_ref, lb_ref = refs[:10]
    src_refs = refs[10:10 + n_jobs]
    o_ref = refs[10 + n_jobs]
    dst_refs = refs[11 + n_jobs:11 + 2 * n_jobs]
    win_ref, rot_ref = refs[11 + 2 * n_jobs:]
    i = pl.program_id(0)
    _cast_jobs(src_refs, dst_refs)

    def glu(a_ref, g_ref):
        return a_ref[...].astype(F32) * jax.nn.sigmoid(g_ref[...].astype(F32))

    win_ref[0:CONV_HALO, :] = jnp.where(i == 0, 0.0, glu(ap_ref, gp_ref))
    win_ref[CONV_HALO:CONV_HALO + tm, :] = glu(ac_ref, gc_ref)
    win_ref[CONV_HALO + tm:, :] = jnp.where(i == pl.num_programs(0) - 1, 0.0, glu(an_ref, gn_ref))

    n_rot = rot_ref.shape[1]
    for s in range(8):
        rot_ref[s] = win_ref[s:s + n_rot, :]

    first = CONV_HALO - CONV_WIDTH // 2
    lanes = 128
    for r in range(tm // rs):
        parts = []
        for c in range(D_CONV // lanes):
            cs = slice(c * lanes, (c + 1) * lanes)
            acc = jnp.zeros((rs, lanes), F32)
            for t in range(CONV_WIDTH):
                off = r * rs + first + t
                base = off - off % 8
                acc = acc + rot_ref[off % 8, base:base + rs, cs] * w_ref[t:t + 1, cs]
            parts.append(acc)
        y = jnp.concatenate(parts, axis=-1) + b_ref[...]
        y = _layer_norm(y, lg_ref[...], lb_ref[...])
        o_ref[r * rs:(r + 1) * rs, :] = (y * jax.nn.sigmoid(y)).astype(BF16)


def _conv_module(zin, w_dw, b_dw, ln_g, ln_b, tm, jobs=(), rs=32):
    n = zin.shape[0]
    nt = n // tm
    hb = tm // CONV_HALO
    n_hb = n // CONV_HALO
    prev_map = lambda col: (lambda i: (jnp.maximum(i * hb - 1, 0), col))
    next_map = lambda col: (lambda i: (jnp.minimum((i + 1) * hb, n_hb - 1), col))
    cur_map = lambda col: (lambda i: (i, col))
    vec = lambda v: v.reshape(1, D_CONV).astype(F32)
    job_specs = [_cast_job_specs(src, layer, cb, nt, lambda i: i) for src, layer, cb in jobs]
    outs = pl.pallas_call(
        functools.partial(_conv_kernel, tm=tm, rs=rs, n_jobs=len(jobs)),
        grid=(nt,),
        in_specs=[
            pl.BlockSpec((CONV_HALO, D_CONV), prev_map(COL_A)),
            pl.BlockSpec((tm, D_CONV), cur_map(COL_A)),
            pl.BlockSpec((CONV_HALO, D_CONV), next_map(COL_A)),
            pl.BlockSpec((CONV_HALO, D_CONV), prev_map(COL_G)),
            pl.BlockSpec((tm, D_CONV), cur_map(COL_G)),
            pl.BlockSpec((CONV_HALO, D_CONV), next_map(COL_G)),
            pl.BlockSpec((CONV_WIDTH, D_CONV), lambda i: (0, 0)),
            pl.BlockSpec((1, D_CONV), lambda i: (0, 0)),
            pl.BlockSpec((1, D_CONV), lambda i: (0, 0)),
            pl.BlockSpec((1, D_CONV), lambda i: (0, 0)),
        ] + [js[0] for js in job_specs],
        out_specs=[pl.BlockSpec((tm, D_CONV), lambda i: (i, 0))] + [js[1] for js in job_specs],
        out_shape=[jax.ShapeDtypeStruct((n, D_CONV), BF16)] + [js[2] for js in job_specs],
        scratch_shapes=[pltpu.VMEM((tm + 2 * CONV_HALO, D_CONV), F32),
                        pltpu.VMEM((8, tm + 2 * CONV_HALO - 8, D_CONV), F32)],
        compiler_params=_cparams(("arbitrary",)),
        name="conv_module",
    )(zin, zin, zin, zin, zin, zin, w_dw.astype(F32), vec(b_dw), vec(ln_g), vec(ln_b),
      *[src for src, _, _ in jobs])
    return outs[0], list(outs[1:])


def _gate_mix_kernel(*refs, n_jobs):
    (h_ref, yna_ref, yf_ref, yc_ref, wg0_ref, wg1_ref, wg2_ref, bg0_ref, bg1_ref, bg2_ref,
     wna_ref, wf_ref, wc_ref) = refs[:13]
    src_refs = refs[13:13 + n_jobs]
    m_ref = refs[13 + n_jobs]
    dst_refs = refs[14 + n_jobs:14 + 2 * n_jobs]
    _cast_jobs(src_refs, dst_refs)
    h = h_ref[...]
    g0 = jax.nn.sigmoid(_dot(h, wg0_ref[0]) + bg0_ref[...])
    m = g0 * _dot(yna_ref[...], wna_ref[0])
    g1 = jax.nn.sigmoid(_dot(h, wg1_ref[0]) + bg1_ref[...])
    m = m + g1 * _dot(yf_ref[...].astype(BF16), wf_ref[0])
    g2 = jax.nn.sigmoid(_dot(h, wg2_ref[0]) + bg2_ref[...])
    m = m + g2 * _dot(yc_ref[...], wc_ref[0])
    m_ref[...] = m.astype(BF16)


def _gate_mix(h, y_na, y_f, y_c, w_gate, b_gate, w_b_na, w_b_f, w_b_c, tm, jobs=()):
    n, d = h.shape
    nc, _, cw = w_b_na.shape
    n_rt = n // tm
    ga = lambda br: (lambda c, i: (br * nc + c, 0, 0))
    gb = lambda br: (lambda c, i: (0, br * nc + c))
    pa = lambda c, i: (c, 0, 0)
    row = lambda c, i: (i, 0)
    bg = b_gate.reshape(1, -1)
    job_specs = [_cast_job_specs(src, layer, cb, nc * n_rt, lambda c, i: c * n_rt + i) for src, layer, cb in jobs]
    outs = pl.pallas_call(
        functools.partial(_gate_mix_kernel, n_jobs=len(jobs)),
        grid=(nc, n_rt),
        in_specs=[
            pl.BlockSpec((tm, d), row),
            pl.BlockSpec((tm, D_NA), row),
            pl.BlockSpec((tm, D_FOURIER), row),
            pl.BlockSpec((tm, D_CONV), row),
            pl.BlockSpec((1, d, cw), ga(0)),
            pl.BlockSpec((1, d, cw), ga(1)),
            pl.BlockSpec((1, d, cw), ga(2)),
            pl.BlockSpec((1, cw), gb(0)),
            pl.BlockSpec((1, cw), gb(1)),
            pl.BlockSpec((1, cw), gb(2)),
            pl.BlockSpec((1, D_NA, cw), pa),
            pl.BlockSpec((1, D_FOURIER, cw), pa),
            pl.BlockSpec((1, D_CONV, cw), pa),
        ] + [js[0] for js in job_specs],
        out_specs=[pl.BlockSpec((tm, cw), lambda c, i: (i, c))] + [js[1] for js in job_specs],
        out_shape=[jax.ShapeDtypeStruct((n, d), BF16)] + [js[2] for js in job_specs],
        compiler_params=_cparams(("arbitrary", "arbitrary")),
        name="gate_mix",
    )(h, y_na, y_f, y_c, w_gate, w_gate, w_gate, bg, bg, bg, w_b_na, w_b_f, w_b_c, *[src for src, _, _ in jobs])
    return outs[0], list(outs[1:])


def _out_proj_kernel(m_ref, x_ref, mod_ref, wo_ref, g_ref, b_ref, o_ref):
    nc, _, cw = wo_ref.shape
    m = m_ref[...]
    for c in range(nc):
        cs = slice(c * cw, (c + 1) * cw)
        o_ref[:, cs] = ALPHA * x_ref[:, cs] + mod_ref[5:6, cs] * _dot(m, wo_ref[c])
    _residual_layer_norm(lambda rows, cs: o_ref[rows, cs], g_ref, b_ref, o_ref, cw, min(o_ref.shape[0], 256))


def _out_proj(m, x, mod, w_o, ln_g, ln_b, tm):
    n, d = x.shape
    row = lambda i: (i, 0)
    fixed = lambda i: (0, 0)
    return pl.pallas_call(
        _out_proj_kernel,
        grid=(n // tm,),
        in_specs=[
            pl.BlockSpec((tm, d), row),
            pl.BlockSpec((tm, d), row),
            pl.BlockSpec(mod.shape, fixed),
            pl.BlockSpec(w_o.shape, lambda i: (0, 0, 0)),
            pl.BlockSpec((1, d), fixed),
            pl.BlockSpec((1, d), fixed),
        ],
        out_specs=pl.BlockSpec((tm, d), row),
        out_shape=jax.ShapeDtypeStruct((n, d), F32),
        compiler_params=_cparams(("parallel",)),
        name="out_proj",
    )(m, x, mod, w_o, ln_g.reshape(1, d), ln_b.reshape(1, d))


def _merge(x, h, mod, y_na, y_f, y_c, w_gate, b_gate, w_b_na, w_b_f, w_b_c, w_o, ln_g, ln_b, tm_mix, tm_out, jobs=()):
    m, cast = _gate_mix(h, y_na, y_f, y_c, w_gate, b_gate, w_b_na, w_b_f, w_b_c, tm=tm_mix, jobs=jobs)
    return _out_proj(m, x, mod, w_o, ln_g, ln_b, tm=tm_out), cast


def _row_tile(n, target):
    t = min(n, target)
    assert n % t == 0
    return t


def kernel(x, c, ctx, c_ctx, w_ada, b_ada, ffn1_w1, ffn1_w3, ffn1_w2, ffn2_w1, ffn2_w3, ffn2_w2, w_in, w_dw, b_dw, conv_ln_g, conv_ln_b, rpb, w_gate, b_gate, w_b_na, w_b_f, w_b_c, w_o, ln_g, ln_b):
    b_, n, d = x.shape
    assert b_ == 1 and c.shape[0] == 1
    n_ctx = ctx.shape[1]
    depth = w_ada.shape[0]

    c_rows = jnp.zeros((8, d), F32).at[0].set(c[0]).at[1].set(c_ctx)
    mods = _ada_mods(c_rows, w_ada, b_ada)

    tm = _row_tile(n, 1024)
    tmc = _row_tile(n_ctx, 256)
    nk1, nk2 = D_FF // FFN1_TF, D_FF // FFN2_TF
    ctx_kb = lambda tf: max(1, CTX_FFN_BYTES // (d * tf * 2))
    n_in = D_IN // INPROJ_TN
    nc = d // MERGE_CW

    ffn_jobs = lambda w1, w3, w2, l, nk: [(w1, l, nk), (w3, l, nk), (w2, l, 1)]
    mix_jobs = lambda l: [(w_in, l, n_in), (w_gate, l, 3 * nc), (w_b_na, l, nc), (w_b_f, l, nc), (w_b_c, l, nc),
                          (w_o, l, nc)]
    ffn1_w = [_blocked_bf16(ffn1_w1[0], nk1), _blocked_bf16(ffn1_w3[0], nk1), _blocked_bf16(ffn1_w2[0], 1)]
    ffn1_tf = FFN1_TF
    mix_w = ffn2_w = None

    xl = x[0]
    xc = ctx[0]
    for l in range(depth):
        last = l == depth - 1
        mod = mods[l, 0].reshape(N_MOD_USED, d)
        modc = mods[l, 1].reshape(N_MOD_USED, d)
        conv_w = (w_dw[l], b_dw[l], conv_ln_g[l], conv_ln_b[l])

        jobs = mix_jobs(l) + ffn_jobs(ffn2_w1, ffn2_w3, ffn2_w2, l, nk2) if l == 0 else []
        xl, cast = _ffn(xl, mod, 0, *ffn1_w, ln_g[l, 0], ln_b[l, 0], tm=tm, jobs=jobs)
        if l == 0:
            mix_w, ffn2_w = cast[:6], cast[6:]
        w_in_l, wg_l, wna_l, wf_l, wc_l, wo_l = mix_w
        xc, _ = _ffn(xc, modc, 0, *ffn1_w, ln_g[l, 0], ln_b[l, 0], tm=tmc, kb=ctx_kb(ffn1_tf))

        zin, hl = _inproj(xl, mod, w_in_l, tm=tm)
        zc, hc = _inproj(xc, modc, w_in_l, tm=tmc)

        merge_w = (wg_l, b_gate[l], wna_l, wf_l, wc_l, wo_l, ln_g[l, 1], ln_b[l, 1])
        next_ffn1 = [] if last else ffn_jobs(ffn1_w1, ffn1_w3, ffn1_w2, l + 1, nk2)
        y_na, next_ffn2_w = _na_attention(zin, zc, rpb[l],
                                          jobs=[] if last else ffn_jobs(ffn2_w1, ffn2_w3, ffn2_w2, l + 1, nk2))
        y_f = _fourier_latent(zin, n1=n // GRID_W, n2=GRID_W)
        y_c, next_mix_w = _conv_module(zin, *conv_w, tm=_row_tile(n, 512), jobs=[] if last else mix_jobs(l + 1))
        xl_new, next_w13 = _merge(xl, hl, mod, y_na, y_f, y_c, *merge_w, tm_mix=tm, tm_out=_row_tile(n, 512),
                                  jobs=next_ffn1[:2])

        if not last:
            yc_na = _ctx_attention(zc)
            yc_f = _fourier_ctx(zc)
            yc_c, _ = _conv_module(zc, *conv_w, tm=n_ctx)
            xc, _ = _merge(xc, hc, modc, yc_na, yc_f, yc_c, *merge_w, tm_mix=tmc, tm_out=tmc)
        xl = xl_new

        xl, next_w2 = _ffn(xl, mod, 1, *ffn2_w, ln_g[l, 2], ln_b[l, 2], tm=tm, jobs=next_ffn1[2:])
        if not last:
            xc, _ = _ffn(xc, modc, 1, *ffn2_w, ln_g[l, 2], ln_b[l, 2], tm=tmc, kb=ctx_kb(FFN2_TF))
            ffn1_w, ffn1_tf = next_w13 + next_w2, FFN2_TF
            mix_w, ffn2_w = next_mix_w, next_ffn2_w
    return xl[None]
```
